```python
import math
import jax, jax.numpy as jnp
from jax import lax
import numpy as np

D_MODEL = 2048
BATCH = 2
SEQ = 8192
DEPTH = 2

CHUNK = 64
Q_BLOCK = 128
HA = 8
A_NOPE = 128
A_ROPE = 64
A_V = 128
Q_RANK = 384
KV_RANK = 128
ROPE_THETA = 10000.0
HB = 4
B_DH = 64
HC = 4
C_DH = 128
LEFT_CHUNKS = 8
REL_CLIP = 256
REL_SIZE = REL_CLIP + CHUNK
MIX_WIDTH = HA * A_V + HB * 2 * B_DH + HC * C_DH
IN_SIZES = (Q_RANK, KV_RANK, A_ROPE,
            HB * 2 * B_DH, HB * 2 * B_DH, HB * 2 * B_DH,
            HC * C_DH, HC * C_DH, HC * C_DH)
IN_WIDTH = sum(IN_SIZES)
D_FF = 4 * D_MODEL
EPS = 1e-6
NEG = -1e30

kernel_name = "hybrid_mla_diff_chunkband_stream_encoder"


def rms_norm(x, g):
    xf = x.astype(jnp.float32)
    y = xf * lax.rsqrt(jnp.mean(xf * xf, axis=-1, keepdims=True) + EPS)
    return (y * g.astype(jnp.float32)).astype(x.dtype)


def apply_rope(x, pos):
    half = x.shape[-1] // 2
    inv_freq = ROPE_THETA ** (-jnp.arange(half, dtype=jnp.float32) / half)
    ang = pos.astype(jnp.float32)[:, None] * inv_freq[None, :]
    cos = jnp.cos(ang)[:, None, :]
    sin = jnp.sin(ang)[:, None, :]
    xf = x.astype(jnp.float32)
    x1, x2 = xf[..., :half], xf[..., half:]
    return jnp.concatenate([x1 * cos - x2 * sin, x2 * cos + x1 * sin], axis=-1).astype(x.dtype)


def chunk_causal_mask(q_pos, seq):
    k_chunk = jnp.arange(seq) // CHUNK
    return k_chunk[None, :] <= (q_pos // CHUNK)[:, None]


def sweep_query_blocks(block_fn, q):
    b, h, s = q.shape[:3]
    nblk = s // Q_BLOCK
    qb = jnp.moveaxis(q.reshape((b, h, nblk, Q_BLOCK) + q.shape[3:]), 2, 0)
    out = lax.map(lambda a: block_fn(a[0], a[1]), (qb, jnp.arange(nblk)))
    return out.transpose(1, 0, 3, 2, 4).reshape(b, s, h * out.shape[-1])


def mla_mixer(c_q, c_kv, k_rope, q_norm, kv_norm, w_uq, w_ukv):
    b, s, _ = c_q.shape
    pos = jnp.arange(s)
    q = (rms_norm(c_q, q_norm) @ w_uq).reshape(b, s, HA, A_NOPE + A_ROPE)
    q = jnp.concatenate([q[..., :A_NOPE], apply_rope(q[..., A_NOPE:], pos)], axis=-1)
    kv = (rms_norm(c_kv, kv_norm) @ w_ukv).reshape(b, s, HA, A_NOPE + A_V)
    kr = apply_rope(k_rope[:, :, None, :], pos)
    k = jnp.concatenate([kv[..., :A_NOPE], jnp.broadcast_to(kr, (b, s, HA, A_ROPE))], axis=-1)
    q = q.transpose(0, 2, 1, 3)
    k = k.transpose(0, 2, 1, 3)
    v = kv[..., A_NOPE:].transpose(0, 2, 1, 3)
    scale = (A_NOPE + A_ROPE) ** -0.5

    def block(qb, blk):
        q_pos = blk * Q_BLOCK + jnp.arange(Q_BLOCK)
        sc = jnp.einsum('bhqd,bhkd->bhqk', qb, k).astype(jnp.float32) * scale
        sc = jnp.where(chunk_causal_mask(q_pos, s), sc, NEG)
        p = jax.nn.softmax(sc, axis=-1).astype(v.dtype)
        return jnp.einsum('bhqk,bhkd->bhqd', p, v)

    return sweep_query_blocks(block, q)


def diff_mixer(q, k, v, lq1, lk1, lq2, lk2, subln, lam_init):
    b, s, _ = q.shape
    q = q.reshape(b, s, HB, 2, B_DH).transpose(0, 2, 1, 3, 4)
    k = k.reshape(b, s, HB, 2, B_DH).transpose(0, 2, 1, 3, 4)
    v = v.reshape(b, s, HB, 2 * B_DH).transpose(0, 2, 1, 3)
    f32 = jnp.float32
    lam = (jnp.exp(jnp.sum(lq1.astype(f32) * lk1.astype(f32)))
           - jnp.exp(jnp.sum(lq2.astype(f32) * lk2.astype(f32))) + lam_init)
    slopes = 2.0 ** (-8.0 * (jnp.arange(HB, dtype=f32) + 1.0) / HB)
    k_pos = jnp.arange(s)
    scale = B_DH ** -0.5

    def block(qb, blk):
        q_pos = blk * Q_BLOCK + jnp.arange(Q_BLOCK)
        sc = jnp.einsum('bhqnd,bhknd->nbhqk', qb, k).astype(f32) * scale
        dist = jnp.abs(q_pos[:, None] - k_pos[None, :]).astype(f32)
        sc = sc - slopes[:, None, None] * dist
        sc = jnp.where(chunk_causal_mask(q_pos, s), sc, NEG)
        p = jax.nn.softmax(sc, axis=-1)
        w = (p[0] - lam * p[1]).astype(v.dtype)
        o = jnp.einsum('bhqk,bhkd->bhqd', w, v)
        return rms_norm(o, subln) * (1.0 - lam_init)

    return sweep_query_blocks(block, q)


def chunk_band_mixer(q, k, v, rel_bias):
    b, s, _ = q.shape
    nc = s // CHUNK
    win = LEFT_CHUNKS + 1
    idx = jnp.arange(nc)[:, None] + jnp.arange(win)[None, :]

    def band(t):
        t = t.reshape(b, nc, CHUNK, HC, C_DH)
        t = jnp.pad(t, ((0, 0), (LEFT_CHUNKS, 0), (0, 0), (0, 0), (0, 0)))
        return t[:, idx].reshape(b, nc, win * CHUNK, HC, C_DH)

    qc = q.reshape(b, nc, CHUNK, HC, C_DH)
    kb, vb = band(k), band(v)
    valid = (jnp.arange(nc)[:, None] - LEFT_CHUNKS + jnp.arange(win)[None, :]) >= 0
    valid = jnp.repeat(valid, CHUNK, axis=1)
    rel = LEFT_CHUNKS * CHUNK + jnp.arange(CHUNK)[:, None] - jnp.arange(win * CHUNK)[None, :]
    rel_idx = jnp.clip(rel, -(CHUNK - 1), REL_CLIP) + (CHUNK - 1)
    bias = rel_bias[:, rel_idx].astype(jnp.float32)
    sc = jnp.einsum('bcqhd,bckhd->bchqk', qc, kb).astype(jnp.float32) * (C_DH ** -0.5)
    sc = sc + bias[None, None]
    sc = jnp.where(valid[None, :, None, None, :], sc, NEG)
    p = jax.nn.softmax(sc, axis=-1).astype(vb.dtype)
    o = jnp.einsum('bchqk,bckhd->bcqhd', p, vb)
    return o.reshape(b, s, HC * C_DH)


def setup_inputs(seed: int = 0) -> dict:
    key = jax.random.key(seed)
    ks = jax.random.split(key, 20)
    f32 = jnp.float32
    nrm = lambda k, shape, s: jax.random.normal(k, shape, f32) * s
    gain = lambda k, shape: 1.0 + 0.02 * jax.random.normal(k, shape, f32)
    return {
        "x": nrm(ks[0], (BATCH, SEQ, D_MODEL), 1.0),
        "attn_norm": gain(ks[1], (DEPTH, D_MODEL)),
        "w_in": nrm(ks[2], (DEPTH, D_MODEL, IN_WIDTH), D_MODEL ** -0.5),
        "q_a_norm": gain(ks[3], (DEPTH, Q_RANK)),
        "kv_a_norm": gain(ks[4], (DEPTH, KV_RANK)),
        "w_uq": nrm(ks[5], (DEPTH, Q_RANK, HA * (A_NOPE + A_ROPE)), Q_RANK ** -0.5),
        "w_ukv": nrm(ks[6], (DEPTH, KV_RANK, HA * (A_NOPE + A_V)), KV_RANK ** -0.5),
        "lambda_q1": nrm(ks[7], (DEPTH, B_DH), 0.1),
        "lambda_k1": nrm(ks[8], (DEPTH, B_DH), 0.1),
        "lambda_q2": nrm(ks[9], (DEPTH, B_DH), 0.1),
        "lambda_k2": nrm(ks[10], (DEPTH, B_DH), 0.1),
        "diff_subln": gain(ks[11], (DEPTH, 2 * B_DH)),
        "rel_bias": nrm(ks[12], (DEPTH, HC, REL_SIZE), 0.1),
        "w_o": nrm(ks[13], (DEPTH, MIX_WIDTH, D_MODEL), MIX_WIDTH ** -0.5),
        "mlp_norm": gain(ks[14], (DEPTH, D_MODEL)),
        "w_up": nrm(ks[15], (DEPTH, D_MODEL, D_FF), D_MODEL ** -0.5),
        "w_down": nrm(ks[16], (DEPTH, D_FF, D_MODEL), D_FF ** -0.5),
        "final_norm": gain(ks[17], (D_MODEL,)),
    }


def reference(x, attn_norm, w_in, q_a_norm, kv_a_norm, w_uq, w_ukv,
              lambda_q1, lambda_k1, lambda_q2, lambda_k2, diff_subln, rel_bias,
              w_o, mlp_norm, w_up, w_down, final_norm):
    split_points = np.cumsum(np.array(IN_SIZES))[:-1]
    for l in range(DEPTH):
        h = rms_norm(x, attn_norm[l])
        proj = h @ w_in[l]
        (c_q, c_kv, k_rope, qb, kb, vb, qc, kc, vc) = jnp.split(proj, split_points, axis=-1)
        out_a = mla_mixer(c_q, c_kv, k_rope, q_a_norm[l], kv_a_norm[l], w_uq[l], w_ukv[l])
        lam_init = 0.8 - 0.6 * math.exp(-0.3 * l)
        out_b = diff_mixer(qb, kb, vb, lambda_q1[l], lambda_k1[l], lambda_q2[l],
                           lambda_k2[l], diff_subln[l], lam_init)
        out_c = chunk_band_mixer(qc, kc, vc, rel_bias[l])
        mixed = jnp.concatenate([out_a, out_b, out_c], axis=-1)
        x = x + mixed @ w_o[l]
        h = rms_norm(x, mlp_norm[l])
        x = x + jnp.square(jax.nn.relu(h @ w_up[l])) @ w_down[l]
    return rms_norm(x, final_norm)
```

```python
import functools
import math

import jax
import jax.numpy as jnp
from jax import lax
from jax.experimental import pallas as pl
from jax.experimental.pallas import tpu as pltpu

F32 = jnp.float32
BF16 = jnp.bfloat16

D_MODEL = 2048
DEPTH = 2
CHUNK = 64
CHUNK_SHIFT = CHUNK.bit_length() - 1
HA, A_NOPE, A_ROPE, A_V = 8, 128, 64, 128
Q_RANK, KV_RANK = 384, 128
ROPE_THETA = 10000.0
HB, B_DH = 4, 64
HC, C_DH = 4, 128
LEFT_CHUNKS = 8
REL_CLIP = 256
REL_SIZE = REL_CLIP + CHUNK
D_FF = 4 * D_MODEL
EPS = 1e-6
NEG = -1e30

LANES = 128
LAT_W = 640
PROJ_W = 6 * 512 + 128
A_QK = 256
VMEM_LIMIT = 56 * 1024 * 1024

NT_DIMS = (((1,), (1,)), ((), ()))


def _params(n_axes, vmem=VMEM_LIMIT):
    return pltpu.CompilerParams(
        dimension_semantics=("arbitrary",) * n_axes, vmem_limit_bytes=vmem)


def _rms(x, g):
    return x * lax.rsqrt(jnp.mean(x * x, axis=-1, keepdims=True) + EPS) * g


def _inproj_kernel(x_ref, g_ref, w_ref, lat_ref, proj_ref, xn_ref, *, rows):
    j = pl.program_id(1)

    @pl.when(j == 0)
    def _():
        for r in range(0, x_ref.shape[0], rows):
            xn_ref[r:r + rows, :] = _rms(x_ref[r:r + rows, :], g_ref[...]).astype(BF16)

    acc = jnp.dot(xn_ref[...], w_ref[...], preferred_element_type=F32)

    @pl.when(j == 0)
    def _():
        lat_ref[...] = acc

    @pl.when(j > 0)
    def _():
        proj_ref[...] = acc.astype(BF16)


def _inproj(x, g, w, *, tm=1024):
    t, d = x.shape
    n_tiles = w.shape[1] // LAT_W
    return pl.pallas_call(
        functools.partial(_inproj_kernel, rows=256),
        grid=(t // tm, n_tiles),
        in_specs=[
            pl.BlockSpec((tm, d), lambda i, j: (i, 0)),
            pl.BlockSpec((1, d), lambda i, j: (0, 0)),
            pl.BlockSpec((d, LAT_W), lambda i, j: (0, j)),
        ],
        out_specs=[
            pl.BlockSpec((tm, LAT_W), lambda i, j: (i, 0)),
            pl.BlockSpec((tm, LAT_W), lambda i, j: (i, jnp.maximum(j - 1, 0))),
        ],
        out_shape=[
            jax.ShapeDtypeStruct((t, LAT_W), F32),
            jax.ShapeDtypeStruct((t, PROJ_W), BF16),
        ],
        scratch_shapes=[pltpu.VMEM((tm, d), BF16)],
        compiler_params=_params(2),
        name="inproj",
    )(x, g, w)


def _mla_prep_kernel(lat_ref, qn_ref, kvn_ref, wuq_ref, wuk_ref, wuv_ref, rope_ref,
                     q_ref, k_ref, v_ref):
    cqn = _rms(lat_ref[:, :Q_RANK], qn_ref[...]).astype(BF16)
    ckvn = _rms(lat_ref[:, Q_RANK:Q_RANK + KV_RANK], kvn_ref[...]).astype(BF16)
    kr = lat_ref[:, Q_RANK + KV_RANK:]
    cos = rope_ref[:, :LANES]
    sin_hi = rope_ref[:, LANES:2 * LANES]
    sin_lo = rope_ref[:, 2 * LANES:]

    def rot(r):
        return (r * cos + pltpu.roll(r, A_ROPE // 2, 1) * sin_hi
                + pltpu.roll(r, LANES - A_ROPE // 2, 1) * sin_lo)

    scale = (A_NOPE + A_ROPE) ** -0.5
    q = jnp.dot(cqn, wuq_ref[...], preferred_element_type=F32)
    kn = jnp.dot(ckvn, wuk_ref[...], preferred_element_type=F32)
    krr = rot(kr).astype(BF16)
    for h in range(HA):
        c0 = A_QK * h
        q_ref[:, c0:c0 + LANES] = (q[:, c0:c0 + LANES] * scale).astype(BF16)
        q_ref[:, c0 + LANES:c0 + A_QK] = (rot(q[:, c0 + LANES:c0 + A_QK]) * scale).astype(BF16)
        k_ref[:, c0:c0 + LANES] = kn[:, A_NOPE * h:A_NOPE * (h + 1)].astype(BF16)
        k_ref[:, c0 + LANES:c0 + A_QK] = krr
    v_ref[...] = jnp.dot(ckvn, wuv_ref[...], preferred_element_type=F32).astype(BF16)


def _mla_prep(lat, qn, kvn, wuq, wuk, wuv, rope, *, seq, tm=512):
    t = lat.shape[0]
    const = lambda i: (0, 0)
    return pl.pallas_call(
        _mla_prep_kernel,
        grid=(t // tm,),
        in_specs=[
            pl.BlockSpec((tm, LAT_W), lambda i: (i, 0)),
            pl.BlockSpec((1, Q_RANK), const),
            pl.BlockSpec((1, KV_RANK), const),
            pl.BlockSpec(wuq.shape, const),
            pl.BlockSpec(wuk.shape, const),
            pl.BlockSpec(wuv.shape, const),
            pl.BlockSpec((tm, 3 * LANES), lambda i: (i % (seq // tm), 0)),
        ],
        out_specs=[
            pl.BlockSpec((tm, HA * A_QK), lambda i: (i, 0)),
            pl.BlockSpec((tm, HA * A_QK), lambda i: (i, 0)),
            pl.BlockSpec((tm, HA * A_V), lambda i: (i, 0)),
        ],
        out_shape=[
            jax.ShapeDtypeStruct((t, HA * A_QK), BF16),
            jax.ShapeDtypeStruct((t, HA * A_QK), BF16),
            jax.ShapeDtypeStruct((t, HA * A_V), BF16),
        ],
        compiler_params=_params(1),
        name="mla_prep",
    )(lat, qn, kvn, wuq, wuk, wuv, rope)


def _softmax_step(s, v, m_ref, l_ref, acc_ref):
    m_prev = m_ref[...]
    m_next = jnp.maximum(m_prev, jnp.max(s, axis=1)[:, None])
    p = jnp.exp(s - jnp.tile(m_next, (1, s.shape[1] // LANES)))
    alpha = jnp.exp(m_prev - m_next)
    l_ref[...] = alpha * l_ref[...] + jnp.sum(p, axis=1)[:, None]
    acc_ref[...] = alpha * acc_ref[...] + jnp.dot(
        p.astype(BF16), v, preferred_element_type=F32)
    m_ref[...] = m_next


def _init_softmax(m_ref, l_ref, acc_ref):
    m_ref[...] = jnp.full(m_ref.shape, NEG, F32)
    l_ref[...] = jnp.zeros(l_ref.shape, F32)
    acc_ref[...] = jnp.zeros(acc_ref.shape, F32)


def _chunk_ids(shape, dim):
    return lax.broadcasted_iota(jnp.int32, shape, dim) >> CHUNK_SHIFT


def _mla_attn_kernel(q_ref, k_ref, v_ref, o_ref, m_ref, l_ref, acc_ref, *, tq):
    i = pl.program_id(2)
    _init_softmax(m_ref, l_ref, acc_ref)
    q = q_ref[...]

    def scores(off):
        k = k_ref[pl.ds(off, tq), :]
        return lax.dot_general(q, k, NT_DIMS, preferred_element_type=F32)

    def body(j, carry):
        off = pl.multiple_of(j * tq, tq)
        _softmax_step(scores(off), v_ref[pl.ds(off, tq), :], m_ref, l_ref, acc_ref)
        return carry

    lax.fori_loop(0, i, body, 0)

    off = pl.multiple_of(i * tq, tq)
    s = scores(off)
    s = jnp.where(_chunk_ids(s.shape, 1) <= _chunk_ids(s.shape, 0), s, NEG)
    _softmax_step(s, v_ref[pl.ds(off, tq), :], m_ref, l_ref, acc_ref)
    o_ref[...] = (acc_ref[...] / l_ref[...]).astype(BF16)


def _mla_attn(q, k, v, *, batch, seq, tq=512):
    nq = seq // tq
    return pl.pallas_call(
        functools.partial(_mla_attn_kernel, tq=tq),
        grid=(batch, HA, nq),
        in_specs=[
            pl.BlockSpec((tq, A_QK), lambda b, h, i: (b * nq + i, h)),
            pl.BlockSpec((seq, A_QK), lambda b, h, i: (b, h)),
            pl.BlockSpec((seq, A_V), lambda b, h, i: (b, h)),
        ],
        out_specs=pl.BlockSpec((tq, A_V), lambda b, h, i: (b * nq + i, h)),
        out_shape=jax.ShapeDtypeStruct((batch * seq, HA * A_V), BF16),
        scratch_shapes=[pltpu.VMEM((tq, LANES), F32)] * 3,
        compiler_params=_params(3),
        name="mla_attn",
    )(q, k, v)


def _diff_attn_kernel(q_ref, k_ref, v_ref, lq1_ref, lk1_ref, lq2_ref, lk2_ref, g_ref,
                      o_ref, qs_ref, m_ref, l_ref, acc_ref, *, tq, lam_init):
    h = pl.program_id(1)
    i = pl.program_id(2)
    _init_softmax(m_ref, l_ref, acc_ref)

    q = q_ref[...].astype(F32) * (B_DH ** -0.5)
    lane = lax.broadcasted_iota(jnp.int32, q.shape, 1)
    qs_ref[:tq, :] = jnp.where(lane < B_DH, q, 0.0).astype(BF16)
    qs_ref[tq:, :] = jnp.where(lane >= B_DH, q, 0.0).astype(BF16)

    slope_bits = (127 - (8 // HB) * (h + 1)) << 23
    slope = lax.bitcast_convert_type(jnp.full((1, tq), slope_bits, jnp.int32), F32)
    col = lax.broadcasted_iota(jnp.int32, (1, tq), 1)

    def scores(off):
        k = k_ref[pl.ds(off, tq), :]
        return lax.dot_general(qs_ref[...], k, NT_DIMS, preferred_element_type=F32)

    def body(j, carry):
        off = pl.multiple_of(j * tq, tq)
        bias = slope * (col + (j - i) * tq).astype(F32)
        _softmax_step(scores(off) + bias, v_ref[pl.ds(off, tq), :], m_ref, l_ref, acc_ref)
        return carry

    lax.fori_loop(0, i, body, 0)

    off = pl.multiple_of(i * tq, tq)
    r = lax.broadcasted_iota(jnp.int32, (tq, tq), 0)
    c = lax.broadcasted_iota(jnp.int32, (tq, tq), 1)
    bias = slope * jnp.minimum(c, 2 * r - c).astype(F32)
    bias = jnp.where((c >> CHUNK_SHIFT) <= (r >> CHUNK_SHIFT), bias, NEG)
    s = scores(off) + jnp.concatenate([bias, bias], axis=0)
    _softmax_step(s, v_ref[pl.ds(off, tq), :], m_ref, l_ref, acc_ref)

    lam = (jnp.exp(jnp.sum(lq1_ref[...] * lk1_ref[...], axis=-1, keepdims=True))
           - jnp.exp(jnp.sum(lq2_ref[...] * lk2_ref[...], axis=-1, keepdims=True))
           + lam_init)
    o1 = acc_ref[:tq, :] / l_ref[:tq, :]
    o2 = acc_ref[tq:, :] / l_ref[tq:, :]
    o = o1 - lam * o2
    o_ref[...] = (_rms(o, g_ref[...]) * (1.0 - lam_init)).astype(BF16)


def _diff_attn(proj, lq1, lk1, lq2, lk2, subln, *, batch, seq, lam_init, tq=256):
    nq = seq // tq
    vec = pl.BlockSpec((1, B_DH), lambda b, h, i: (0, 0))
    return pl.pallas_call(
        functools.partial(_diff_attn_kernel, tq=tq, lam_init=lam_init),
        grid=(batch, HB, nq),
        in_specs=[
            pl.BlockSpec((tq, LANES), lambda b, h, i: (b * nq + i, h)),
            pl.BlockSpec((seq, LANES), lambda b, h, i: (b, HB + h)),
            pl.BlockSpec((seq, LANES), lambda b, h, i: (b, 2 * HB + h)),
            vec, vec, vec, vec,
            pl.BlockSpec((1, 2 * B_DH), lambda b, h, i: (0, 0)),
        ],
        out_specs=pl.BlockSpec((tq, LANES), lambda b, h, i: (b * nq + i, h)),
        out_shape=jax.ShapeDtypeStruct((batch * seq, HB * 2 * B_DH), BF16),
        scratch_shapes=[pltpu.VMEM((2 * tq, LANES), BF16)]
        + [pltpu.VMEM((2 * tq, LANES), F32)] * 3,
        compiler_params=_params(3),
        name="diff_attn",
    )(proj, proj, proj, lq1, lk1, lq2, lk2, subln)


def _band_attn_kernel(q_ref, k_ref, v_ref, base_ref, o_ref, bias_ref, *, gq):
    b = pl.program_id(1)
    g = pl.program_id(2)
    win = 2 * gq

    @pl.when(jnp.logical_and(b == 0, g == 0))
    def _():
        x = jnp.broadcast_to(base_ref[0], (gq, win))
        x = pltpu.roll(x, 0, 1, stride=1, stride_axis=0)
        d = _chunk_ids((gq, win), 1) - _chunk_ids((gq, win), 0)
        bias_ref[...] = jnp.where(jnp.logical_and(d >= 0, d <= LEFT_CHUNKS), x, NEG)

    q = q_ref[...]

    def attend(k, v, bias):
        s = lax.dot_general(q, k, NT_DIMS, preferred_element_type=F32)
        s = s * (C_DH ** -0.5) + bias
        p = jnp.exp(s - jnp.max(s, axis=-1, keepdims=True))
        l = jnp.sum(p, axis=-1, keepdims=True)
        o = jnp.dot(p.astype(BF16), v, preferred_element_type=F32)
        o_ref[...] = (o / l).astype(BF16)

    @pl.when(g == 0)
    def _():
        attend(k_ref[:gq, :], v_ref[:gq, :], bias_ref[:, gq:])

    @pl.when(g > 0)
    def _():
        off = pl.multiple_of((g - 1) * gq, gq)
        attend(k_ref[pl.ds(off, win), :], v_ref[pl.ds(off, win), :], bias_ref[...])


def _band_attn(proj, base, *, batch, seq):
    gq = LEFT_CHUNKS * CHUNK
    ng = seq // gq
    qc, kc, vc = 3 * HB, 3 * HB + HC, 3 * HB + 2 * HC
    return pl.pallas_call(
        functools.partial(_band_attn_kernel, gq=gq),
        grid=(HC, batch, ng),
        in_specs=[
            pl.BlockSpec((gq, C_DH), lambda h, b, g: (b * ng + g, qc + h)),
            pl.BlockSpec((seq, C_DH), lambda h, b, g: (b, kc + h)),
            pl.BlockSpec((seq, C_DH), lambda h, b, g: (b, vc + h)),
            pl.BlockSpec((1, 1, 2 * gq), lambda h, b, g: (h, 0, 0)),
        ],
        out_specs=pl.BlockSpec((gq, C_DH), lambda h, b, g: (b * ng + g, h)),
        out_shape=jax.ShapeDtypeStruct((batch * seq, HC * C_DH), BF16),
        scratch_shapes=[pltpu.VMEM((gq, 2 * gq), F32)],
        compiler_params=_params(3),
        name="band_attn",
    )(proj, proj, proj, base)


def _oproj_kernel(a_ref, b_ref, c_ref, x_ref, wa_ref, wb_ref, wc_ref, o_ref):
    acc = jnp.dot(a_ref[...], wa_ref[...], preferred_element_type=F32)
    acc += jnp.dot(b_ref[...], wb_ref[...], preferred_element_type=F32)
    acc += jnp.dot(c_ref[...], wc_ref[...], preferred_element_type=F32)
    o_ref[...] = x_ref[...] + acc


def _oproj(a, b, c, x, wa, wb, wc, *, tm=1024, tn=1024):
    t, d = x.shape
    row = lambda i, j: (i, 0)
    col = lambda i, j: (0, j)
    return pl.pallas_call(
        _oproj_kernel,
        grid=(t // tm, d // tn),
        in_specs=[
            pl.BlockSpec((tm, a.shape[1]), row),
            pl.BlockSpec((tm, b.shape[1]), row),
            pl.BlockSpec((tm, c.shape[1]), row),
            pl.BlockSpec((tm, tn), lambda i, j: (i, j)),
            pl.BlockSpec((wa.shape[0], tn), col),
            pl.BlockSpec((wb.shape[0], tn), col),
            pl.BlockSpec((wc.shape[0], tn), col),
        ],
        out_specs=pl.BlockSpec((tm, tn), lambda i, j: (i, j)),
        out_shape=jax.ShapeDtypeStruct((t, d), F32),
        compiler_params=_params(2),
        name="oproj",
    )(a, b, c, x, wa, wb, wc)


def _mlp_kernel(x_ref, g_ref, wu_ref, wd_ref, gf_ref, o_ref, xn_ref, *, rows, final_norm):
    f = pl.program_id(1)

    @pl.when(f == 0)
    def _():
        for r in range(0, x_ref.shape[0], rows):
            xr = x_ref[r:r + rows, :]
            xn_ref[r:r + rows, :] = _rms(xr, g_ref[...]).astype(BF16)
            o_ref[r:r + rows, :] = xr

    hidden = jnp.dot(xn_ref[...], wu_ref[...], preferred_element_type=F32)
    hidden = jnp.square(jnp.maximum(hidden, 0.0)).astype(BF16)
    o_ref[...] += jnp.dot(hidden, wd_ref[...], preferred_element_type=F32)

    if final_norm:
        @pl.when(f == pl.num_programs(1) - 1)
        def _():
            for r in range(0, x_ref.shape[0], rows):
                o_ref[r:r + rows, :] = _rms(o_ref[r:r + rows, :], gf_ref[...])


def _mlp(x, g, wu, wd, gf, *, final_norm, tm=1024, tf=512):
    t, d = x.shape
    dff = wu.shape[1]
    return pl.pallas_call(
        functools.partial(_mlp_kernel, rows=256, final_norm=final_norm),
        grid=(t // tm, dff // tf),
        in_specs=[
            pl.BlockSpec((tm, d), lambda i, f: (i, 0)),
            pl.BlockSpec((1, d), lambda i, f: (0, 0)),
            pl.BlockSpec((d, tf), lambda i, f: (0, f)),
            pl.BlockSpec((tf, d), lambda i, f: (f, 0)),
            pl.BlockSpec((1, d), lambda i, f: (0, 0)),
        ],
        out_specs=pl.BlockSpec((tm, d), lambda i, f: (i, 0)),
        out_shape=jax.ShapeDtypeStruct((t, d), F32),
        scratch_shapes=[pltpu.VMEM((tm, d), BF16)],
        compiler_params=_params(2),
        name="mlp",
    )(x, g, wu, wd, gf)


def _rope_tables(seq):
    half = A_ROPE // 2
    inv_freq = ROPE_THETA ** (-jnp.arange(half, dtype=F32) / half)
    ang = jnp.arange(seq, dtype=F32)[:, None] * inv_freq[None, :]
    cos, sin = jnp.cos(ang), jnp.sin(ang)
    z = jnp.zeros_like(cos)
    pad = jnp.zeros((seq, LANES - A_ROPE), F32)
    return jnp.concatenate([cos, cos, pad, z, sin, pad, -sin, z, pad], axis=1)


def _pad_in_weight(w_in):
    d = w_in.shape[0]
    lat = Q_RANK + KV_RANK + A_ROPE
    return jnp.concatenate([
        w_in[:, :lat], jnp.zeros((d, LAT_W - lat), w_in.dtype),
        w_in[:, lat:], jnp.zeros((d, PROJ_W - 6 * 512), w_in.dtype)], axis=1).astype(BF16)


def _band_base(rel_bias):
    gq = LEFT_CHUNKS * CHUNK
    edge = rel_bias[:, REL_SIZE - 1:]
    head = jnp.broadcast_to(edge, (HC, gq - REL_CLIP))
    tail = jnp.broadcast_to(edge, (HC, 2 * gq - (gq - REL_CLIP) - REL_SIZE))
    return jnp.concatenate([head, rel_bias[:, ::-1], tail], axis=1).reshape(HC, 1, 2 * gq)


def kernel(x, attn_norm, w_in, q_a_norm, kv_a_norm, w_uq, w_ukv, lambda_q1, lambda_k1,
           lambda_q2, lambda_k2, diff_subln, rel_bias, w_o, mlp_norm, w_up, w_down, final_norm):
    batch, seq, d = x.shape
    xf = x.reshape(batch * seq, d)
    rope = _rope_tables(seq)
    row = lambda v: v.reshape(1, -1)
    for l in range(DEPTH):
        lat, proj = _inproj(xf, row(attn_norm[l]), _pad_in_weight(w_in[l]))

        wuq = w_uq[l].reshape(Q_RANK, HA, A_NOPE + A_ROPE)
        wuq = jnp.pad(wuq, ((0, 0), (0, 0), (0, A_QK - A_NOPE - A_ROPE)))
        wuq = wuq.reshape(Q_RANK, HA * A_QK).astype(BF16)
        wukv = w_ukv[l].reshape(KV_RANK, HA, A_NOPE + A_V)
        wuk = wukv[:, :, :A_NOPE].reshape(KV_RANK, HA * A_NOPE).astype(BF16)
        wuv = wukv[:, :, A_NOPE:].reshape(KV_RANK, HA * A_V).astype(BF16)
        q, k, v = _mla_prep(lat, row(q_a_norm[l]), row(kv_a_norm[l]), wuq, wuk, wuv, rope,
                            seq=seq)
        out_a = _mla_attn(q, k, v, batch=batch, seq=seq)

        lam_init = 0.8 - 0.6 * math.exp(-0.3 * l)
        out_b = _diff_attn(proj, row(lambda_q1[l]), row(lambda_k1[l]), row(lambda_q2[l]),
                           row(lambda_k2[l]), row(diff_subln[l]),
                           batch=batch, seq=seq, lam_init=lam_init)
        out_c = _band_attn(proj, _band_base(rel_bias[l]), batch=batch, seq=seq)

        wo = w_o[l].astype(BF16)
        na, nb = HA * A_V, HA * A_V + HB * 2 * B_DH
        xf = _oproj(out_a, out_b, out_c, xf, wo[:na], wo[na:nb], wo[nb:])
        xf = _mlp(xf, row(mlp_norm[l]), w_up[l].astype(BF16), w_down[l].astype(BF16),
                  row(final_norm), final_norm=(l == DEPTH - 1))
    return xf.reshape(batch, seq, d)
```

```python
import functools
import math

import jax
import jax.numpy as jnp
from jax import lax
from jax.experimental import pallas as pl
from jax.experimental.pallas import tpu as pltpu

F32 = jnp.float32
BF16 = jnp.bfloat16

D_MODEL = 2048
DEPTH = 2
CHUNK = 64
CHUNK_SHIFT = CHUNK.bit_length() - 1
HA, A_NOPE, A_ROPE, A_V = 8, 128, 64, 128
Q_RANK, KV_RANK = 384, 128
ROPE_THETA = 10000.0
HB, B_DH = 4, 64
HC, C_DH = 4, 128
LEFT_CHUNKS = 8
REL_CLIP = 256
REL_SIZE = REL_CLIP + CHUNK
D_FF = 4 * D_MODEL
EPS = 1e-6
NEG = -1e30

LANES = 128
LAT_W = 640
PROJ_W = 6 * 512 + 128
A_QK = 256
MLA_TQ = 128
FLASH_TK = 512
LOG2E = math.log2(math.e)
VMEM_LIMIT = 56 * 1024 * 1024

NT_DIMS = (((1,), (1,)), ((), ()))


def _params(n_axes, vmem=VMEM_LIMIT):
    return pltpu.CompilerParams(
        dimension_semantics=("arbitrary",) * n_axes, vmem_limit_bytes=vmem)


def _rms(x, g):
    return x * lax.rsqrt(jnp.mean(x * x, axis=-1, keepdims=True) + EPS) * g


def _chunk_ids(shape, dim):
    return lax.broadcasted_iota(jnp.int32, shape, dim) >> CHUNK_SHIFT


def _inproj_kernel(x_ref, g_ref, w_ref, lat_ref, proj_ref, xn_ref, *, rows):
    j = pl.program_id(1)

    @pl.when(j == 0)
    def _():
        for r in range(0, x_ref.shape[0], rows):
            xn_ref[r:r + rows, :] = _rms(x_ref[r:r + rows, :], g_ref[...]).astype(BF16)

    acc = jnp.dot(xn_ref[...], w_ref[...], preferred_element_type=F32)

    @pl.when(j == 0)
    def _():
        lat_ref[...] = acc

    @pl.when(j > 0)
    def _():
        proj_ref[...] = acc.astype(BF16)


def _inproj(x, g, w, *, tm=1024):
    t, d = x.shape
    n_tiles = w.shape[1] // LAT_W
    return pl.pallas_call(
        functools.partial(_inproj_kernel, rows=256),
        grid=(t // tm, n_tiles),
        in_specs=[
            pl.BlockSpec((tm, d), lambda i, j: (i, 0)),
            pl.BlockSpec((1, d), lambda i, j: (0, 0)),
            pl.BlockSpec((d, LAT_W), lambda i, j: (0, j)),
        ],
        out_specs=[
            pl.BlockSpec((tm, LAT_W), lambda i, j: (i, 0)),
            pl.BlockSpec((tm, LAT_W), lambda i, j: (i, jnp.maximum(j - 1, 0))),
        ],
        out_shape=[
            jax.ShapeDtypeStruct((t, LAT_W), F32),
            jax.ShapeDtypeStruct((t, PROJ_W), BF16),
        ],
        scratch_shapes=[pltpu.VMEM((tm, d), BF16)],
        compiler_params=_params(2),
        name="inproj",
    )(x, g, w)


def _mla_prep_kernel(lat_ref, qn_ref, kvn_ref, wuq_ref, wukt_ref, rope_ref,
                     q_ref, k_ref, v_ref):
    tm = lat_ref.shape[0]
    cqn = _rms(lat_ref[:, :Q_RANK], qn_ref[...]).astype(BF16)
    ckvn = _rms(lat_ref[:, Q_RANK:Q_RANK + KV_RANK], kvn_ref[...]).astype(BF16)
    kr = lat_ref[:, Q_RANK + KV_RANK:]
    cos = rope_ref[:, :LANES]
    sin_hi = rope_ref[:, LANES:2 * LANES]
    sin_lo = rope_ref[:, 2 * LANES:]

    def rot(r):
        return (r * cos + pltpu.roll(r, A_ROPE // 2, 1) * sin_hi
                + pltpu.roll(r, LANES - A_ROPE // 2, 1) * sin_lo)

    scale = (A_NOPE + A_ROPE) ** -0.5 * LOG2E
    q = jnp.dot(cqn, wuq_ref[...], preferred_element_type=F32)
    for h in range(HA):
        c0 = A_QK * h
        qa = jnp.dot(q[:, c0:c0 + LANES].astype(BF16), wukt_ref[h],
                     preferred_element_type=F32)
        qa = (qa * scale).astype(BF16)
        qr = (rot(q[:, c0 + LANES:c0 + A_QK]) * scale).astype(BF16)
        for blk in range(tm // MLA_TQ):
            r0 = (blk * HA + h) * MLA_TQ
            q_ref[r0:r0 + MLA_TQ, :LANES] = qa[blk * MLA_TQ:(blk + 1) * MLA_TQ]
            q_ref[r0:r0 + MLA_TQ, LANES:] = qr[blk * MLA_TQ:(blk + 1) * MLA_TQ]
    k_ref[:, :LANES] = ckvn
    k_ref[:, LANES:] = rot(kr).astype(BF16)
    v_ref[:, :LANES] = ckvn
    v_ref[:, LANES:] = jnp.ones((tm, LANES), BF16)


def _mla_prep(lat, qn, kvn, wuq, wukt, rope, *, seq, tm=512):
    t = lat.shape[0]
    const = lambda i: (0, 0)
    return pl.pallas_call(
        _mla_prep_kernel,
        grid=(t // tm,),
        in_specs=[
            pl.BlockSpec((tm, LAT_W), lambda i: (i, 0)),
            pl.BlockSpec((1, Q_RANK), const),
            pl.BlockSpec((1, KV_RANK), const),
            pl.BlockSpec(wuq.shape, const),
            pl.BlockSpec(wukt.shape, lambda i: (0, 0, 0)),
            pl.BlockSpec((tm, 3 * LANES), lambda i: (i % (seq // tm), 0)),
        ],
        out_specs=[
            pl.BlockSpec((tm * HA, A_QK), lambda i: (i, 0)),
            pl.BlockSpec((tm, A_QK), lambda i: (i, 0)),
            pl.BlockSpec((tm, A_QK), lambda i: (i, 0)),
        ],
        out_shape=[
            jax.ShapeDtypeStruct((t * HA, A_QK), BF16),
            jax.ShapeDtypeStruct((t, A_QK), BF16),
            jax.ShapeDtypeStruct((t, A_QK), BF16),
        ],
        compiler_params=_params(1),
        name="mla_prep",
    )(lat, qn, kvn, wuq, wukt, rope)


def _flash_blocks(n, q_ref, k_ref, v_ref, sa_ref, sb_ref, m_ref, acc_ref, last_bias, exp_fn):
    tk = sa_ref.shape[1]

    def qk(t, dst_ref):
        off = pl.multiple_of(t * tk, tk)
        dst_ref[...] = lax.dot_general(q_ref[...], k_ref[pl.ds(off, tk), :], NT_DIMS,
                                       preferred_element_type=F32)

    def softmax_pv(t, src_ref, bias=None):
        s = src_ref[...]
        if bias is not None:
            s = s + bias
        m_prev = m_ref[...]
        m_next = jnp.maximum(m_prev, jnp.max(s, axis=1)[:, None])
        p = exp_fn(s - jnp.tile(m_next, (1, tk // LANES)))
        alpha = exp_fn(m_prev - m_next)
        off = pl.multiple_of(t * tk, tk)
        pv = jnp.dot(p.astype(BF16), v_ref[pl.ds(off, tk), :], preferred_element_type=F32)
        acc_ref[...] = jnp.tile(alpha, (1, 2)) * acc_ref[...] + pv
        m_ref[...] = m_next

    m_ref[...] = jnp.full(m_ref.shape, NEG, F32)
    acc_ref[...] = jnp.zeros(acc_ref.shape, F32)
    qk(0, sa_ref)

    def pair(p, carry):
        t = 2 * p
        qk(t + 1, sb_ref)
        softmax_pv(t, sa_ref)
        qk(t + 2, sa_ref)
        softmax_pv(t + 1, sb_ref)
        return carry

    lax.fori_loop(0, n >> 1, pair, 0)

    @pl.when((n & 1) == 0)
    def _():
        softmax_pv(n, sa_ref, last_bias())

    @pl.when((n & 1) == 1)
    def _():
        qk(n, sb_ref)
        softmax_pv(n - 1, sa_ref)
        softmax_pv(n, sb_ref, last_bias())


def _flash_scratch(rows):
    return [pltpu.VMEM((rows, FLASH_TK), F32), pltpu.VMEM((rows, FLASH_TK), F32),
            pltpu.VMEM((rows, LANES), F32), pltpu.VMEM((rows, 2 * LANES), F32)]


def _mla_attn_kernel(q_ref, k_ref, v_ref, wuv_ref, o_ref, sa_ref, sb_ref, m_ref, acc_ref):
    i = pl.program_id(1)
    n = (i * MLA_TQ) // FLASH_TK

    def last_bias():
        shape = (MLA_TQ, FLASH_TK)
        rows = i * MLA_TQ + lax.broadcasted_iota(jnp.int32, shape, 0)
        cols = n * FLASH_TK + lax.broadcasted_iota(jnp.int32, shape, 1)
        bias = jnp.where((cols >> CHUNK_SHIFT) <= (rows >> CHUNK_SHIFT), 0.0, NEG)
        return jnp.tile(bias, (HA, 1))

    _flash_blocks(n, q_ref, k_ref, v_ref, sa_ref, sb_ref, m_ref, acc_ref, last_bias, jnp.exp2)

    for h in range(HA):
        a = acc_ref[h * MLA_TQ:(h + 1) * MLA_TQ, :]
        o = (a[:, :LANES] / a[:, LANES:]).astype(BF16)
        o_ref[:, h * A_V:(h + 1) * A_V] = jnp.dot(
            o, wuv_ref[h], preferred_element_type=F32).astype(BF16)


def _mla_attn(q, k, v, wuv, *, batch, seq):
    nq = seq // MLA_TQ
    rows = HA * MLA_TQ
    return pl.pallas_call(
        _mla_attn_kernel,
        grid=(batch, nq),
        in_specs=[
            pl.BlockSpec((rows, A_QK), lambda b, i: (b * nq + i, 0)),
            pl.BlockSpec((seq, A_QK), lambda b, i: (b, 0)),
            pl.BlockSpec((seq, A_QK), lambda b, i: (b, 0)),
            pl.BlockSpec(wuv.shape, lambda b, i: (0, 0, 0)),
        ],
        out_specs=pl.BlockSpec((MLA_TQ, HA * A_V), lambda b, i: (b * nq + i, 0)),
        out_shape=jax.ShapeDtypeStruct((batch * seq, HA * A_V), BF16),
        scratch_shapes=_flash_scratch(rows),
        compiler_params=_params(2),
        name="mla_attn",
    )(q, k, v, wuv)


def _diff_attn_kernel(q_ref, k_ref, v_ref, lq1_ref, lk1_ref, lq2_ref, lk2_ref, g_ref, o_ref,
                      kaug_ref, vaug_ref, qs_ref, sa_ref, sb_ref, m_ref, acc_ref, *, lam_init):
    h = pl.program_id(1)
    i = pl.program_id(2)
    tq = q_ref.shape[0]
    lane = lax.broadcasted_iota(jnp.int32, (tq, LANES), 1)

    @pl.when(i == 0)
    def _():
        def fill(c, carry):
            off = pl.multiple_of(c * tq, tq)
            pos = off + lax.broadcasted_iota(jnp.int32, (tq, LANES), 0)
            cols = jnp.where(lane == 0, pos >> CHUNK_SHIFT,
                             jnp.where(lane == 1, pos & (CHUNK - 1), 0))
            kaug_ref[pl.ds(off, tq), :LANES] = k_ref[pl.ds(off, tq), :]
            kaug_ref[pl.ds(off, tq), LANES:] = cols.astype(F32).astype(BF16)
            vaug_ref[pl.ds(off, tq), :LANES] = v_ref[pl.ds(off, tq), :]
            vaug_ref[pl.ds(off, tq), LANES:] = jnp.ones((tq, LANES), BF16)
            return carry

        lax.fori_loop(0, k_ref.shape[0] // tq, fill, 0)

    slope_bits = (127 - (8 // HB) * (h + 1)) << 23
    slope = lax.bitcast_convert_type(jnp.full((1, LANES), slope_bits, jnp.int32), F32)
    pos_cols = jnp.where(lane == 0, slope * CHUNK, jnp.where(lane == 1, slope, 0.0)).astype(BF16)
    q = q_ref[...].astype(F32) * (B_DH ** -0.5)
    qs_ref[:tq, :LANES] = jnp.where(lane < B_DH, q, 0.0).astype(BF16)
    qs_ref[tq:, :LANES] = jnp.where(lane >= B_DH, q, 0.0).astype(BF16)
    qs_ref[:tq, LANES:] = pos_cols
    qs_ref[tq:, LANES:] = pos_cols

    def last_bias():
        r = lax.broadcasted_iota(jnp.int32, (tq, tq), 0)
        c = lax.broadcasted_iota(jnp.int32, (tq, tq), 1)
        ahead = jnp.maximum(c - r, 0).astype(F32)
        bias = jnp.tile(slope, (1, tq // LANES)) * (-2.0 * ahead)
        bias = jnp.where((c >> CHUNK_SHIFT) <= (r >> CHUNK_SHIFT), bias, NEG)
        return jnp.tile(bias, (2, 1))

    _flash_blocks(i, qs_ref, kaug_ref, vaug_ref, sa_ref, sb_ref, m_ref, acc_ref,
                  last_bias, jnp.exp)

    lam = (jnp.exp(jnp.sum(lq1_ref[...] * lk1_ref[...], axis=-1, keepdims=True))
           - jnp.exp(jnp.sum(lq2_ref[...] * lk2_ref[...], axis=-1, keepdims=True))
           + lam_init)
    o1 = acc_ref[:tq, :LANES] / acc_ref[:tq, LANES:]
    o2 = acc_ref[tq:, :LANES] / acc_ref[tq:, LANES:]
    o = o1 - lam * o2
    o_ref[...] = (_rms(o, g_ref[...]) * (1.0 - lam_init)).astype(BF16)


def _diff_attn(proj, lq1, lk1, lq2, lk2, subln, *, batch, seq, lam_init):
    tq = FLASH_TK
    nq = seq // tq
    vec = pl.BlockSpec((1, B_DH), lambda b, h, i: (0, 0))
    return pl.pallas_call(
        functools.partial(_diff_attn_kernel, lam_init=lam_init),
        grid=(batch, HB, nq),
        in_specs=[
            pl.BlockSpec((tq, LANES), lambda b, h, i: (b * nq + i, h)),
            pl.BlockSpec((seq, LANES), lambda b, h, i: (b, HB + h)),
            pl.BlockSpec((seq, LANES), lambda b, h, i: (b, 2 * HB + h)),
            vec, vec, vec, vec,
            pl.BlockSpec((1, 2 * B_DH), lambda b, h, i: (0, 0)),
        ],
        out_specs=pl.BlockSpec((tq, LANES), lambda b, h, i: (b * nq + i, h)),
        out_shape=jax.ShapeDtypeStruct((batch * seq, HB * 2 * B_DH), BF16),
        scratch_shapes=[pltpu.VMEM((seq, 2 * LANES), BF16), pltpu.VMEM((seq, 2 * LANES), BF16),
                        pltpu.VMEM((2 * tq, 2 * LANES), BF16)] + _flash_scratch(2 * tq),
        compiler_params=_params(3),
        name="diff_attn",
    )(proj, proj, proj, lq1, lk1, lq2, lk2, subln)


def _band_attn_kernel(q_ref, k_ref, v_ref, base_ref, o_ref, bias_ref, *, gq):
    b = pl.program_id(1)
    g = pl.program_id(2)
    win = 2 * gq

    @pl.when(jnp.logical_and(b == 0, g == 0))
    def _():
        x = jnp.broadcast_to(base_ref[0], (gq, win))
        x = pltpu.roll(x, 0, 1, stride=1, stride_axis=0)
        d = _chunk_ids((gq, win), 1) - _chunk_ids((gq, win), 0)
        bias_ref[...] = jnp.where(jnp.logical_and(d >= 0, d <= LEFT_CHUNKS), x, NEG)

    q = q_ref[...]

    def attend(k, v, bias):
        s = lax.dot_general(q, k, NT_DIMS, preferred_element_type=F32)
        s = s * (C_DH ** -0.5) + bias
        p = jnp.exp(s - jnp.max(s, axis=-1, keepdims=True))
        l = jnp.sum(p, axis=-1, keepdims=True)
        o = jnp.dot(p.astype(BF16), v, preferred_element_type=F32)
        o_ref[...] = (o / l).astype(BF16)

    @pl.when(g == 0)
    def _():
        attend(k_ref[:gq, :], v_ref[:gq, :], bias_ref[:, gq:])

    @pl.when(g > 0)
    def _():
        off = pl.multiple_of((g - 1) * gq, gq)
        attend(k_ref[pl.ds(off, win), :], v_ref[pl.ds(off, win), :], bias_ref[...])


def _band_attn(proj, base, *, batch, seq):
    gq = LEFT_CHUNKS * CHUNK
    ng = seq // gq
    qc, kc, vc = 3 * HB, 3 * HB + HC, 3 * HB + 2 * HC
    return pl.pallas_call(
        functools.partial(_band_attn_kernel, gq=gq),
        grid=(HC, batch, ng),
        in_specs=[
            pl.BlockSpec((gq, C_DH), lambda h, b, g: (b * ng + g, qc + h)),
            pl.BlockSpec((seq, C_DH), lambda h, b, g: (b, kc + h)),
            pl.BlockSpec((seq, C_DH), lambda h, b, g: (b, vc + h)),
            pl.BlockSpec((1, 1, 2 * gq), lambda h, b, g: (h, 0, 0)),
        ],
        out_specs=pl.BlockSpec((gq, C_DH), lambda h, b, g: (b * ng + g, h)),
        out_shape=jax.ShapeDtypeStruct((batch * seq, HC * C_DH), BF16),
        scratch_shapes=[pltpu.VMEM((gq, 2 * gq), F32)],
        compiler_params=_params(3),
        name="band_attn",
    )(proj, proj, proj, base)


def _oproj_kernel(a_ref, b_ref, c_ref, x_ref, wa_ref, wb_ref, wc_ref, o_ref):
    acc = jnp.dot(a_ref[...], wa_ref[...], preferred_element_type=F32)
    acc += jnp.dot(b_ref[...], wb_ref[...], preferred_element_type=F32)
    acc += jnp.dot(c_ref[...], wc_ref[...], preferred_element_type=F32)
    o_ref[...] = x_ref[...] + acc


def _oproj(a, b, c, x, wa, wb, wc, *, tm=1024, tn=1024):
    t, d = x.shape
    row = lambda i, j: (i, 0)
    col = lambda i, j: (0, j)
    return pl.pallas_call(
        _oproj_kernel,
        grid=(t // tm, d // tn),
        in_specs=[
            pl.BlockSpec((tm, a.shape[1]), row),
            pl.BlockSpec((tm, b.shape[1]), row),
            pl.BlockSpec((tm, c.shape[1]), row),
            pl.BlockSpec((tm, tn), lambda i, j: (i, j)),
            pl.BlockSpec((wa.shape[0], tn), col),
            pl.BlockSpec((wb.shape[0], tn), col),
            pl.BlockSpec((wc.shape[0], tn), col),
        ],
        out_specs=pl.BlockSpec((tm, tn), lambda i, j: (i, j)),
        out_shape=jax.ShapeDtypeStruct((t, d), F32),
        compiler_params=_params(2),
        name="oproj",
    )(a, b, c, x, wa, wb, wc)


def _mlp_kernel(x_ref, g_ref, wu_ref, wd_ref, gf_ref, o_ref, xn_ref, *, rows, final_norm):
    f = pl.program_id(1)

    @pl.when(f == 0)
    def _():
        for r in range(0, x_ref.shape[0], rows):
            xr = x_ref[r:r + rows, :]
            xn_ref[r:r + rows, :] = _rms(xr, g_ref[...]).astype(BF16)
            o_ref[r:r + rows, :] = xr

    hidden = jnp.dot(xn_ref[...], wu_ref[...], preferred_element_type=F32)
    hidden = jnp.square(jnp.maximum(hidden, 0.0)).astype(BF16)
    o_ref[...] += jnp.dot(hidden, wd_ref[...], preferred_element_type=F32)

    if final_norm:
        @pl.when(f == pl.num_programs(1) - 1)
        def _():
            for r in range(0, x_ref.shape[0], rows):
                o_ref[r:r + rows, :] = _rms(o_ref[r:r + rows, :], gf_ref[...])


def _mlp(x, g, wu, wd, gf, *, final_norm, tm=1024, tf=512):
    t, d = x.shape
    dff = wu.shape[1]
    return pl.pallas_call(
        functools.partial(_mlp_kernel, rows=256, final_norm=final_norm),
        grid=(t // tm, dff // tf),
        in_specs=[
            pl.BlockSpec((tm, d), lambda i, f: (i, 0)),
            pl.BlockSpec((1, d), lambda i, f: (0, 0)),
            pl.BlockSpec((d, tf), lambda i, f: (0, f)),
            pl.BlockSpec((tf, d), lambda i, f: (f, 0)),
            pl.BlockSpec((1, d), lambda i, f: (0, 0)),
        ],
        out_specs=pl.BlockSpec((tm, d), lambda i, f: (i, 0)),
        out_shape=jax.ShapeDtypeStruct((t, d), F32),
        scratch_shapes=[pltpu.VMEM((tm, d), BF16)],
        compiler_params=_params(2),
        name="mlp",
    )(x, g, wu, wd, gf)


def _rope_tables(seq):
    half = A_ROPE // 2
    inv_freq = ROPE_THETA ** (-jnp.arange(half, dtype=F32) / half)
    ang = jnp.arange(seq, dtype=F32)[:, None] * inv_freq[None, :]
    cos, sin = jnp.cos(ang), jnp.sin(ang)
    z = jnp.zeros_like(cos)
    pad = jnp.zeros((seq, LANES - A_ROPE), F32)
    return jnp.concatenate([cos, cos, pad, z, sin, pad, -sin, z, pad], axis=1)


def _pad_in_weight(w_in):
    d = w_in.shape[0]
    lat = Q_RANK + KV_RANK + A_ROPE
    return jnp.concatenate([
        w_in[:, :lat], jnp.zeros((d, LAT_W - lat), w_in.dtype),
        w_in[:, lat:], jnp.zeros((d, PROJ_W - 6 * 512), w_in.dtype)], axis=1).astype(BF16)


def _band_base(rel_bias):
    gq = LEFT_CHUNKS * CHUNK
    edge = rel_bias[:, REL_SIZE - 1:]
    head = jnp.broadcast_to(edge, (HC, gq - REL_CLIP))
    tail = jnp.broadcast_to(edge, (HC, 2 * gq - (gq - REL_CLIP) - REL_SIZE))
    return jnp.concatenate([head, rel_bias[:, ::-1], tail], axis=1).reshape(HC, 1, 2 * gq)


def kernel(x, attn_norm, w_in, q_a_norm, kv_a_norm, w_uq, w_ukv, lambda_q1, lambda_k1,
           lambda_q2, lambda_k2, diff_subln, rel_bias, w_o, mlp_norm, w_up, w_down, final_norm):
    batch, seq, d = x.shape
    xf = x.reshape(batch * seq, d)
    rope = _rope_tables(seq)
    row = lambda v: v.reshape(1, -1)
    for l in range(DEPTH):
        lat, proj = _inproj(xf, row(attn_norm[l]), _pad_in_weight(w_in[l]))

        wuq = w_uq[l].reshape(Q_RANK, HA, A_NOPE + A_ROPE)
        wuq = jnp.pad(wuq, ((0, 0), (0, 0), (0, A_QK - A_NOPE - A_ROPE)))
        wuq = wuq.reshape(Q_RANK, HA * A_QK).astype(BF16)
        wukv = w_ukv[l].reshape(KV_RANK, HA, A_NOPE + A_V)
        wukt = wukv[:, :, :A_NOPE].transpose(1, 2, 0).astype(BF16)
        wuv = wukv[:, :, A_NOPE:].transpose(1, 0, 2).astype(BF16)
        q, k, v = _mla_prep(lat, row(q_a_norm[l]), row(kv_a_norm[l]), wuq, wukt, rope, seq=seq)
        out_a = _mla_attn(q, k, v, wuv, batch=batch, seq=seq)

        lam_init = 0.8 - 0.6 * math.exp(-0.3 * l)
        out_b = _diff_attn(proj, row(lambda_q1[l]), row(lambda_k1[l]), row(lambda_q2[l]),
                           row(lambda_k2[l]), row(diff_subln[l]),
                           batch=batch, seq=seq, lam_init=lam_init)
        out_c = _band_attn(proj, _band_base(rel_bias[l]), batch=batch, seq=seq)

        wo = w_o[l].astype(BF16)
        na, nb = HA * A_V, HA * A_V + HB * 2 * B_DH
        xf = _oproj(out_a, out_b, out_c, xf, wo[:na], wo[na:nb], wo[nb:])
        xf = _mlp(xf, row(mlp_norm[l]), w_up[l].astype(BF16), w_down[l].astype(BF16),
                  row(final_norm), final_norm=(l == DEPTH - 1))
    return xf.reshape(batch, seq, d)
```

```python
import functools
import math

import jax
import jax.numpy as jnp
from jax import lax
from jax.experimental import pallas as pl
from jax.experimental.pallas import tpu as pltpu

F32 = jnp.float32
BF16 = jnp.bfloat16

D_MODEL = 2048
DEPTH = 2
CHUNK = 64
CHUNK_SHIFT = CHUNK.bit_length() - 1
HA, A_NOPE, A_ROPE, A_V = 8, 128, 64, 128
Q_RANK, KV_RANK = 384, 128
ROPE_THETA = 10000.0
HB, B_DH = 4, 64
HC, C_DH = 4, 128
LEFT_CHUNKS = 8
REL_CLIP = 256
REL_SIZE = REL_CLIP + CHUNK
D_FF = 4 * D_MODEL
EPS = 1e-6
NEG = -1e30

LANES = 128
LAT_W = 768
PROJ_W = 6 * 512
A_QK = 256
MLA_TQ = 128
FLASH_TK = 512
LOG2E = math.log2(math.e)
VMEM_LIMIT = 56 * 1024 * 1024

NT_DIMS = (((1,), (1,)), ((), ()))


def _params(n_axes, vmem=VMEM_LIMIT):
    return pltpu.CompilerParams(
        dimension_semantics=("arbitrary",) * n_axes, vmem_limit_bytes=vmem)


def _rms(x, g):
    return x * lax.rsqrt(jnp.mean(x * x, axis=-1, keepdims=True) + EPS) * g


def _chunk_ids(shape, dim):
    return lax.broadcasted_iota(jnp.int32, shape, dim) >> CHUNK_SHIFT


def _inproj_kernel(x_ref, g_ref, w_ref, lat_ref, proj_ref, xn_ref, *, rows):
    j = pl.program_id(1)

    @pl.when(j == 0)
    def _():
        for r in range(0, x_ref.shape[0], rows):
            xn_ref[r:r + rows, :] = _rms(x_ref[r:r + rows, :], g_ref[...]).astype(BF16)

    acc = jnp.dot(xn_ref[...], w_ref[...], preferred_element_type=F32)

    @pl.when(j == 0)
    def _():
        lat_ref[...] = acc

    @pl.when(j > 0)
    def _():
        proj_ref[...] = acc.astype(BF16)


def _inproj(x, g, w, *, tm=1024):
    t, d = x.shape
    n_tiles = w.shape[1] // LAT_W
    return pl.pallas_call(
        functools.partial(_inproj_kernel, rows=256),
        grid=(t // tm, n_tiles),
        in_specs=[
            pl.BlockSpec((tm, d), lambda i, j: (i, 0)),
            pl.BlockSpec((1, d), lambda i, j: (0, 0)),
            pl.BlockSpec((d, LAT_W), lambda i, j: (0, j)),
        ],
        out_specs=[
            pl.BlockSpec((tm, LAT_W), lambda i, j: (i, 0)),
            pl.BlockSpec((tm, LAT_W), lambda i, j: (i, jnp.maximum(j - 1, 0))),
        ],
        out_shape=[
            jax.ShapeDtypeStruct((t, LAT_W), F32),
            jax.ShapeDtypeStruct((t, PROJ_W), BF16),
        ],
        scratch_shapes=[pltpu.VMEM((tm, d), BF16)],
        compiler_params=_params(2),
        name="inproj",
    )(x, g, w)


def _mla_prep_kernel(lat_ref, qn_ref, kvn_ref, wuq_ref, wukt_ref, rope_ref,
                     q_ref, k_ref, v_ref):
    tm = lat_ref.shape[0]
    cqn = _rms(lat_ref[:, :Q_RANK], qn_ref[...]).astype(BF16)
    ckvn = _rms(lat_ref[:, Q_RANK:Q_RANK + KV_RANK], kvn_ref[...]).astype(BF16)
    kr = lat_ref[:, Q_RANK + KV_RANK:Q_RANK + KV_RANK + LANES]
    cos = rope_ref[:, :LANES]
    sin_hi = rope_ref[:, LANES:2 * LANES]
    sin_lo = rope_ref[:, 2 * LANES:]

    def rot(r):
        return (r * cos + pltpu.roll(r, A_ROPE // 2, 1) * sin_hi
                + pltpu.roll(r, LANES - A_ROPE // 2, 1) * sin_lo)

    scale = (A_NOPE + A_ROPE) ** -0.5 * LOG2E
    q = jnp.dot(cqn, wuq_ref[...], preferred_element_type=F32)
    for h in range(HA):
        c0 = A_QK * h
        qa = jnp.dot(q[:, c0:c0 + LANES].astype(BF16), wukt_ref[h],
                     preferred_element_type=F32)
        qa = (qa * scale).astype(BF16)
        qr = (rot(q[:, c0 + LANES:c0 + A_QK]) * scale).astype(BF16)
        for blk in range(tm // MLA_TQ):
            r0 = (blk * HA + h) * MLA_TQ
            q_ref[r0:r0 + MLA_TQ, :LANES] = qa[blk * MLA_TQ:(blk + 1) * MLA_TQ]
            q_ref[r0:r0 + MLA_TQ, LANES:] = qr[blk * MLA_TQ:(blk + 1) * MLA_TQ]
    k_ref[:, :LANES] = ckvn
    k_ref[:, LANES:] = rot(kr).astype(BF16)
    v_ref[:, :LANES] = ckvn
    v_ref[:, LANES:] = jnp.ones((tm, LANES), BF16)


def _mla_prep(lat, qn, kvn, wuq, wukt, rope, *, seq, tm=512):
    t = lat.shape[0]
    const = lambda i: (0, 0)
    return pl.pallas_call(
        _mla_prep_kernel,
        grid=(t // tm,),
        in_specs=[
            pl.BlockSpec((tm, LAT_W), lambda i: (i, 0)),
            pl.BlockSpec((1, Q_RANK), const),
            pl.BlockSpec((1, KV_RANK), const),
            pl.BlockSpec(wuq.shape, const),
            pl.BlockSpec(wukt.shape, lambda i: (0, 0, 0)),
            pl.BlockSpec((tm, 3 * LANES), lambda i: (i % (seq // tm), 0)),
        ],
        out_specs=[
            pl.BlockSpec((tm * HA, A_QK), lambda i: (i, 0)),
            pl.BlockSpec((tm, A_QK), lambda i: (i, 0)),
            pl.BlockSpec((tm, A_QK), lambda i: (i, 0)),
        ],
        out_shape=[
            jax.ShapeDtypeStruct((t * HA, A_QK), BF16),
            jax.ShapeDtypeStruct((t, A_QK), BF16),
            jax.ShapeDtypeStruct((t, A_QK), BF16),
        ],
        compiler_params=_params(1),
        name="mla_prep",
    )(lat, qn, kvn, wuq, wukt, rope)


def _flash_blocks(n, q_ref, k_ref, v_ref, sa_ref, sb_ref, m_ref, acc_ref, last_bias, exp_fn):
    tk = sa_ref.shape[1]

    def qk(t, dst_ref):
        off = pl.multiple_of(t * tk, tk)
        dst_ref[...] = lax.dot_general(q_ref[...], k_ref[pl.ds(off, tk), :], NT_DIMS,
                                       preferred_element_type=F32)

    def softmax_pv(t, src_ref, bias=None):
        s = src_ref[...]
        if bias is not None:
            s = s + bias
        m_prev = m_ref[...]
        m_next = jnp.maximum(m_prev, jnp.max(s, axis=1)[:, None])
        p = exp_fn(s - jnp.tile(m_next, (1, tk // LANES)))
        alpha = exp_fn(m_prev - m_next)
        off = pl.multiple_of(t * tk, tk)
        pv = jnp.dot(p.astype(BF16), v_ref[pl.ds(off, tk), :], preferred_element_type=F32)
        acc_ref[...] = jnp.tile(alpha, (1, 2)) * acc_ref[...] + pv
        m_ref[...] = m_next

    m_ref[...] = jnp.full(m_ref.shape, NEG, F32)
    acc_ref[...] = jnp.zeros(acc_ref.shape, F32)
    qk(0, sa_ref)

    def pair(t):
        qk(t + 1, sb_ref)
        softmax_pv(t, sa_ref)
        qk(t + 2, sa_ref)
        softmax_pv(t + 1, sb_ref)

    def quad(j, carry):
        pair(4 * j)
        pair(4 * j + 2)
        return carry

    lax.fori_loop(0, n >> 2, quad, 0)

    @pl.when((n & 2) != 0)
    def _():
        pair(n & ~3)

    @pl.when((n & 1) == 0)
    def _():
        softmax_pv(n, sa_ref, last_bias())

    @pl.when((n & 1) == 1)
    def _():
        qk(n, sb_ref)
        softmax_pv(n - 1, sa_ref)
        softmax_pv(n, sb_ref, last_bias())


def _flash_scratch(rows):
    return [pltpu.VMEM((rows, FLASH_TK), F32), pltpu.VMEM((rows, FLASH_TK), F32),
            pltpu.VMEM((rows, LANES), F32), pltpu.VMEM((rows, 2 * LANES), F32)]


def _mla_attn_kernel(q_ref, k_ref, v_ref, wuv_ref, o_ref, sa_ref, sb_ref, m_ref, acc_ref):
    i = pl.program_id(1)
    n = (i * MLA_TQ) // FLASH_TK

    def last_bias():
        shape = (MLA_TQ, FLASH_TK)
        rows = i * MLA_TQ + lax.broadcasted_iota(jnp.int32, shape, 0)
        cols = n * FLASH_TK + lax.broadcasted_iota(jnp.int32, shape, 1)
        bias = jnp.where((cols >> CHUNK_SHIFT) <= (rows >> CHUNK_SHIFT), 0.0, NEG)
        return jnp.tile(bias, (HA, 1))

    _flash_blocks(n, q_ref, k_ref, v_ref, sa_ref, sb_ref, m_ref, acc_ref, last_bias, jnp.exp2)

    for h in range(HA):
        a = acc_ref[h * MLA_TQ:(h + 1) * MLA_TQ, :]
        o = (a[:, :LANES] / a[:, LANES:]).astype(BF16)
        o_ref[:, h * A_V:(h + 1) * A_V] = jnp.dot(
            o, wuv_ref[h], preferred_element_type=F32).astype(BF16)


def _mla_attn(q, k, v, wuv, *, batch, seq):
    nq = seq // MLA_TQ
    rows = HA * MLA_TQ
    return pl.pallas_call(
        _mla_attn_kernel,
        grid=(batch, nq),
        in_specs=[
            pl.BlockSpec((rows, A_QK), lambda b, i: (b * nq + i, 0)),
            pl.BlockSpec((seq, A_QK), lambda b, i: (b, 0)),
            pl.BlockSpec((seq, A_QK), lambda b, i: (b, 0)),
            pl.BlockSpec(wuv.shape, lambda b, i: (0, 0, 0)),
        ],
        out_specs=pl.BlockSpec((MLA_TQ, HA * A_V), lambda b, i: (b * nq + i, 0)),
        out_shape=jax.ShapeDtypeStruct((batch * seq, HA * A_V), BF16),
        scratch_shapes=_flash_scratch(rows),
        compiler_params=_params(2),
        name="mla_attn",
    )(q, k, v, wuv)


def _diff_attn_kernel(q_ref, k_ref, v_ref, lq1_ref, lk1_ref, lq2_ref, lk2_ref, g_ref, o_ref,
                      kaug_ref, vaug_ref, qs_ref, sa_ref, sb_ref, m_ref, acc_ref, *, lam_init):
    h = pl.program_id(1)
    i = pl.program_id(2)
    tq = q_ref.shape[0]
    lane = lax.broadcasted_iota(jnp.int32, (tq, LANES), 1)

    @pl.when(i == 0)
    def _():
        def fill(c, carry):
            off = pl.multiple_of(c * tq, tq)
            pos = off + lax.broadcasted_iota(jnp.int32, (tq, LANES), 0)
            cols = jnp.where(lane == 0, pos >> CHUNK_SHIFT,
                             jnp.where(lane == 1, pos & (CHUNK - 1), 0))
            kaug_ref[pl.ds(off, tq), :LANES] = k_ref[pl.ds(off, tq), :]
            kaug_ref[pl.ds(off, tq), LANES:] = cols.astype(F32).astype(BF16)
            vaug_ref[pl.ds(off, tq), :LANES] = v_ref[pl.ds(off, tq), :]
            vaug_ref[pl.ds(off, tq), LANES:] = jnp.ones((tq, LANES), BF16)
            return carry

        lax.fori_loop(0, k_ref.shape[0] // tq, fill, 0)

    slope_bits = (127 - (8 // HB) * (h + 1)) << 23
    slope = lax.bitcast_convert_type(jnp.full((1, LANES), slope_bits, jnp.int32), F32)
    pos_cols = jnp.where(lane == 0, slope * CHUNK, jnp.where(lane == 1, slope, 0.0)).astype(BF16)
    q = q_ref[...].astype(F32) * (B_DH ** -0.5)
    qs_ref[:tq, :LANES] = jnp.where(lane < B_DH, q, 0.0).astype(BF16)
    qs_ref[tq:, :LANES] = jnp.where(lane >= B_DH, q, 0.0).astype(BF16)
    qs_ref[:tq, LANES:] = pos_cols
    qs_ref[tq:, LANES:] = pos_cols

    def last_bias():
        r = lax.broadcasted_iota(jnp.int32, (tq, tq), 0)
        c = lax.broadcasted_iota(jnp.int32, (tq, tq), 1)
        ahead = jnp.maximum(c - r, 0).astype(F32)
        bias = jnp.tile(slope, (1, tq // LANES)) * (-2.0 * ahead)
        bias = jnp.where((c >> CHUNK_SHIFT) <= (r >> CHUNK_SHIFT), bias, NEG)
        return jnp.tile(bias, (2, 1))

    _flash_blocks(i, qs_ref, kaug_ref, vaug_ref, sa_ref, sb_ref, m_ref, acc_ref,
                  last_bias, jnp.exp)

    lam = (jnp.exp(jnp.sum(lq1_ref[...] * lk1_ref[...], axis=-1, keepdims=True))
           - jnp.exp(jnp.sum(lq2_ref[...] * lk2_ref[...], axis=-1, keepdims=True))
           + lam_init)
    o1 = acc_ref[:tq, :LANES] / acc_ref[:tq, LANES:]
    o2 = acc_ref[tq:, :LANES] / acc_ref[tq:, LANES:]
    o = o1 - lam * o2
    o_ref[...] = (_rms(o, g_ref[...]) * (1.0 - lam_init)).astype(BF16)


def _diff_attn(proj, lq1, lk1, lq2, lk2, subln, *, batch, seq, lam_init):
    tq = FLASH_TK
    nq = seq // tq
    vec = pl.BlockSpec((1, B_DH), lambda b, h, i: (0, 0))
    return pl.pallas_call(
        functools.partial(_diff_attn_kernel, lam_init=lam_init),
        grid=(batch, HB, nq),
        in_specs=[
            pl.BlockSpec((tq, LANES), lambda b, h, i: (b * nq + i, h)),
            pl.BlockSpec((seq, LANES), lambda b, h, i: (b, HB + h)),
            pl.BlockSpec((seq, LANES), lambda b, h, i: (b, 2 * HB + h)),
            vec, vec, vec, vec,
            pl.BlockSpec((1, 2 * B_DH), lambda b, h, i: (0, 0)),
        ],
        out_specs=pl.BlockSpec((tq, LANES), lambda b, h, i: (b * nq + i, h)),
        out_shape=jax.ShapeDtypeStruct((batch * seq, HB * 2 * B_DH), BF16),
        scratch_shapes=[pltpu.VMEM((seq, 2 * LANES), BF16), pltpu.VMEM((seq, 2 * LANES), BF16),
                        pltpu.VMEM((2 * tq, 2 * LANES), BF16)] + _flash_scratch(2 * tq),
        compiler_params=_params(3),
        name="diff_attn",
    )(proj, proj, proj, lq1, lk1, lq2, lk2, subln)


def _band_attn_kernel(q_ref, k_ref, v_ref, base_ref, o_ref, bias_ref, *, gq):
    b = pl.program_id(1)
    g2 = pl.program_id(2)
    win = 2 * gq

    @pl.when(jnp.logical_and(b == 0, g2 == 0))
    def _():
        x = jnp.broadcast_to(base_ref[0], (gq, win))
        x = pltpu.roll(x, 0, 1, stride=1, stride_axis=0)
        d = _chunk_ids((gq, win), 1) - _chunk_ids((gq, win), 0)
        bias_ref[...] = jnp.where(jnp.logical_and(d >= 0, d <= LEFT_CHUNKS), x * LOG2E, NEG)

    def attend(half, k, v, bias):
        rows = slice(half * gq, (half + 1) * gq)
        q = (q_ref[rows, :].astype(F32) * (C_DH ** -0.5 * LOG2E)).astype(BF16)
        s = lax.dot_general(q, k, NT_DIMS, preferred_element_type=F32) + bias
        p = jnp.exp2(s - jnp.max(s, axis=-1, keepdims=True))
        l = jnp.sum(p, axis=-1, keepdims=True)
        o = jnp.dot(p.astype(BF16), v, preferred_element_type=F32)
        o_ref[rows, :] = (o / l).astype(BF16)

    @pl.when(g2 == 0)
    def _():
        attend(0, k_ref[:gq, :], v_ref[:gq, :], bias_ref[:, gq:])
        attend(1, k_ref[:win, :], v_ref[:win, :], bias_ref[...])

    @pl.when(g2 > 0)
    def _():
        off0 = pl.multiple_of((2 * g2 - 1) * gq, gq)
        off1 = pl.multiple_of(2 * g2 * gq, gq)
        attend(0, k_ref[pl.ds(off0, win), :], v_ref[pl.ds(off0, win), :], bias_ref[...])
        attend(1, k_ref[pl.ds(off1, win), :], v_ref[pl.ds(off1, win), :], bias_ref[...])


def _band_attn(proj, base, *, batch, seq):
    gq = LEFT_CHUNKS * CHUNK
    ng = seq // (2 * gq)
    qc, kc, vc = 3 * HB, 3 * HB + HC, 3 * HB + 2 * HC
    return pl.pallas_call(
        functools.partial(_band_attn_kernel, gq=gq),
        grid=(HC, batch, ng),
        in_specs=[
            pl.BlockSpec((2 * gq, C_DH), lambda h, b, g: (b * ng + g, qc + h)),
            pl.BlockSpec((seq, C_DH), lambda h, b, g: (b, kc + h)),
            pl.BlockSpec((seq, C_DH), lambda h, b, g: (b, vc + h)),
            pl.BlockSpec((1, 1, 2 * gq), lambda h, b, g: (h, 0, 0)),
        ],
        out_specs=pl.BlockSpec((2 * gq, C_DH), lambda h, b, g: (b * ng + g, h)),
        out_shape=jax.ShapeDtypeStruct((batch * seq, HC * C_DH), BF16),
        scratch_shapes=[pltpu.VMEM((gq, 2 * gq), F32)],
        compiler_params=_params(3),
        name="band_attn",
    )(proj, proj, proj, base)


def _oproj_kernel(a_ref, b_ref, c_ref, x_ref, wa_ref, wb_ref, wc_ref, o_ref):
    acc = jnp.dot(a_ref[...], wa_ref[...], preferred_element_type=F32)
    acc += jnp.dot(b_ref[...], wb_ref[...], preferred_element_type=F32)
    acc += jnp.dot(c_ref[...], wc_ref[...], preferred_element_type=F32)
    o_ref[...] = x_ref[...] + acc


def _oproj(a, b, c, x, wa, wb, wc, *, tm=1024, tn=1024):
    t, d = x.shape
    row = lambda i, j: (i, 0)
    col = lambda i, j: (0, j)
    return pl.pallas_call(
        _oproj_kernel,
        grid=(t // tm, d // tn),
        in_specs=[
            pl.BlockSpec((tm, a.shape[1]), row),
            pl.BlockSpec((tm, b.shape[1]), row),
            pl.BlockSpec((tm, c.shape[1]), row),
            pl.BlockSpec((tm, tn), lambda i, j: (i, j)),
            pl.BlockSpec((wa.shape[0], tn), col),
            pl.BlockSpec((wb.shape[0], tn), col),
            pl.BlockSpec((wc.shape[0], tn), col),
        ],
        out_specs=pl.BlockSpec((tm, tn), lambda i, j: (i, j)),
        out_shape=jax.ShapeDtypeStruct((t, d), F32),
        compiler_params=_params(2),
        name="oproj",
    )(a, b, c, x, wa, wb, wc)


def _mlp_kernel(x_ref, g_ref, wu_ref, wd_ref, gf_ref, o_ref, xn_ref, *, rows, final_norm):
    f = pl.program_id(1)

    @pl.when(f == 0)
    def _():
        for r in range(0, x_ref.shape[0], rows):
            xr = x_ref[r:r + rows, :]
            xn_ref[r:r + rows, :] = _rms(xr, g_ref[...]).astype(BF16)
            o_ref[r:r + rows, :] = xr

    hidden = jnp.dot(xn_ref[...], wu_ref[...], preferred_element_type=F32)
    hidden = jnp.square(jnp.maximum(hidden, 0.0)).astype(BF16)
    o_ref[...] += jnp.dot(hidden, wd_ref[...], preferred_element_type=F32)

    if final_norm:
        @pl.when(f == pl.num_programs(1) - 1)
        def _():
            for r in range(0, x_ref.shape[0], rows):
                o_ref[r:r + rows, :] = _rms(o_ref[r:r + rows, :], gf_ref[...])


def _mlp(x, g, wu, wd, gf, *, final_norm, tm=1024, tf=512):
    t, d = x.shape
    dff = wu.shape[1]
    return pl.pallas_call(
        functools.partial(_mlp_kernel, rows=256, final_norm=final_norm),
        grid=(t // tm, dff // tf),
        in_specs=[
            pl.BlockSpec((tm, d), lambda i, f: (i, 0)),
            pl.BlockSpec((1, d), lambda i, f: (0, 0)),
            pl.BlockSpec((d, tf), lambda i, f: (0, f)),
            pl.BlockSpec((tf, d), lambda i, f: (f, 0)),
            pl.BlockSpec((1, d), lambda i, f: (0, 0)),
        ],
        out_specs=pl.BlockSpec((tm, d), lambda i, f: (i, 0)),
        out_shape=jax.ShapeDtypeStruct((t, d), F32),
        scratch_shapes=[pltpu.VMEM((tm, d), BF16)],
        compiler_params=_params(2),
        name="mlp",
    )(x, g, wu, wd, gf)


def _rope_tables(seq):
    half = A_ROPE // 2
    inv_freq = ROPE_THETA ** (-jnp.arange(half, dtype=F32) / half)
    ang = jnp.arange(seq, dtype=F32)[:, None] * inv_freq[None, :]
    cos, sin = jnp.cos(ang), jnp.sin(ang)
    z = jnp.zeros_like(cos)
    pad = jnp.zeros((seq, LANES - A_ROPE), F32)
    return jnp.concatenate([cos, cos, pad, z, sin, pad, -sin, z, pad], axis=1)


def _pad_in_weight(w_in):
    d = w_in.shape[0]
    lat = Q_RANK + KV_RANK + A_ROPE
    w = w_in.astype(BF16)
    return jnp.concatenate([w[:, :lat], jnp.zeros((d, LAT_W - lat), BF16), w[:, lat:]], axis=1)


def _band_base(rel_bias):
    gq = LEFT_CHUNKS * CHUNK
    edge = rel_bias[:, REL_SIZE - 1:]
    head = jnp.broadcast_to(edge, (HC, gq - REL_CLIP))
    tail = jnp.broadcast_to(edge, (HC, 2 * gq - (gq - REL_CLIP) - REL_SIZE))
    return jnp.concatenate([head, rel_bias[:, ::-1], tail], axis=1).reshape(HC, 1, 2 * gq)


def kernel(x, attn_norm, w_in, q_a_norm, kv_a_norm, w_uq, w_ukv, lambda_q1, lambda_k1,
           lambda_q2, lambda_k2, diff_subln, rel_bias, w_o, mlp_norm, w_up, w_down, final_norm):
    batch, seq, d = x.shape
    xf = x.reshape(batch * seq, d)
    rope = _rope_tables(seq)
    row = lambda v: v.reshape(1, -1)
    for l in range(DEPTH):
        lat, proj = _inproj(xf, row(attn_norm[l]), _pad_in_weight(w_in[l]))

        wuq = w_uq[l].reshape(Q_RANK, HA, A_NOPE + A_ROPE)
        wuq = jnp.pad(wuq, ((0, 0), (0, 0), (0, A_QK - A_NOPE - A_ROPE)))
        wuq = wuq.reshape(Q_RANK, HA * A_QK).astype(BF16)
        wukv = w_ukv[l].reshape(KV_RANK, HA, A_NOPE + A_V)
        wukt = wukv[:, :, :A_NOPE].transpose(1, 2, 0).astype(BF16)
        wuv = wukv[:, :, A_NOPE:].transpose(1, 0, 2).astype(BF16)
        q, k, v = _mla_prep(lat, row(q_a_norm[l]), row(kv_a_norm[l]), wuq, wukt, rope, seq=seq)
        out_a = _mla_attn(q, k, v, wuv, batch=batch, seq=seq)

        lam_init = 0.8 - 0.6 * math.exp(-0.3 * l)
        out_b = _diff_attn(proj, row(lambda_q1[l]), row(lambda_k1[l]), row(lambda_q2[l]),
                           row(lambda_k2[l]), row(diff_subln[l]),
                           batch=batch, seq=seq, lam_init=lam_init)
        out_c = _band_attn(proj, _band_base(rel_bias[l]), batch=batch, seq=seq)

        wo = w_o[l].astype(BF16)
        na, nb = HA * A_V, HA * A_V + HB * 2 * B_DH
        xf = _oproj(out_a, out_b, out_c, xf, wo[:na], wo[na:nb], wo[nb:])
        xf = _mlp(xf, row(mlp_norm[l]), w_up[l].astype(BF16), w_down[l].astype(BF16),
                  row(final_norm), final_norm=(l == DEPTH - 1))
    return xf.reshape(batch, seq, d)
```

```python
import functools
import math

import jax
import jax.numpy as jnp
from jax import lax
from jax.experimental import pallas as pl
from jax.experimental.pallas import tpu as pltpu

F32 = jnp.float32
BF16 = jnp.bfloat16

D_MODEL = 2048
DEPTH = 2
CHUNK = 64
CHUNK_SHIFT = CHUNK.bit_length() - 1
HA, A_NOPE, A_ROPE, A_V = 8, 128, 64, 128
Q_RANK, KV_RANK = 384, 128
ROPE_THETA = 10000.0
HB, B_DH = 4, 64
HC, C_DH = 4, 128
LEFT_CHUNKS = 8
REL_CLIP = 256
REL_SIZE = REL_CLIP + CHUNK
D_FF = 4 * D_MODEL
EPS = 1e-6
NEG = -1e30

LANES = 128
LAT_W = 768
PROJ_W = 6 * 512
A_QK = 256
MLA_TQ = 256
FLASH_TK = 512
LOG2E = math.log2(math.e)
VMEM_LIMIT = 56 * 1024 * 1024

NT_DIMS = (((1,), (1,)), ((), ()))


def _params(n_axes, vmem=VMEM_LIMIT):
    return pltpu.CompilerParams(
        dimension_semantics=("arbitrary",) * n_axes, vmem_limit_bytes=vmem)


def _rms(x, g):
    return x * lax.rsqrt(jnp.mean(x * x, axis=-1, keepdims=True) + EPS) * g


def _chunk_ids(shape, dim):
    return lax.broadcasted_iota(jnp.int32, shape, dim) >> CHUNK_SHIFT


def _inproj_kernel(x_ref, g_ref, wl_ref, wp_ref, lat_ref, proj_ref, xn_ref, *, rows):
    j = pl.program_id(1)

    @pl.when(j == 0)
    def _():
        for r in range(0, x_ref.shape[0], rows):
            xn_ref[r:r + rows, :] = _rms(x_ref[r:r + rows, :], g_ref[...]).astype(BF16)
        lat_ref[...] = jnp.dot(xn_ref[...], wl_ref[...], preferred_element_type=F32)

    @pl.when(j > 0)
    def _():
        proj_ref[...] = jnp.dot(xn_ref[...], wp_ref[...],
                                preferred_element_type=F32).astype(BF16)


def _inproj(x, g, w_lat, w_proj, *, tm=1024):
    t, d = x.shape
    n_tiles = 1 + w_proj.shape[1] // LAT_W
    return pl.pallas_call(
        functools.partial(_inproj_kernel, rows=256),
        grid=(t // tm, n_tiles),
        in_specs=[
            pl.BlockSpec((tm, d), lambda i, j: (i, 0)),
            pl.BlockSpec((1, d), lambda i, j: (0, 0)),
            pl.BlockSpec((d, LAT_W), lambda i, j: (0, 0)),
            pl.BlockSpec((d, LAT_W), lambda i, j: (0, jnp.maximum(j - 1, 0))),
        ],
        out_specs=[
            pl.BlockSpec((tm, LAT_W), lambda i, j: (i, 0)),
            pl.BlockSpec((tm, LAT_W), lambda i, j: (i, jnp.maximum(j - 1, 0))),
        ],
        out_shape=[
            jax.ShapeDtypeStruct((t, LAT_W), F32),
            jax.ShapeDtypeStruct((t, PROJ_W), BF16),
        ],
        scratch_shapes=[pltpu.VMEM((tm, d), BF16)],
        compiler_params=_params(2),
        name="inproj",
    )(x, g, w_lat, w_proj)


def _mla_prep_kernel(lat_ref, qn_ref, kvn_ref, wuq_ref, wukt_ref, rope_ref,
                     q_ref, k_ref, v_ref):
    tm = lat_ref.shape[0]
    cqn = _rms(lat_ref[:, :Q_RANK], qn_ref[...]).astype(BF16)
    ckvn = _rms(lat_ref[:, Q_RANK:Q_RANK + KV_RANK], kvn_ref[...]).astype(BF16)
    kr = lat_ref[:, Q_RANK + KV_RANK:Q_RANK + KV_RANK + LANES]
    cos = rope_ref[:, :LANES]
    sin_hi = rope_ref[:, LANES:2 * LANES]
    sin_lo = rope_ref[:, 2 * LANES:]

    def rot(r):
        return (r * cos + pltpu.roll(r, A_ROPE // 2, 1) * sin_hi
                + pltpu.roll(r, LANES - A_ROPE // 2, 1) * sin_lo)

    scale = (A_NOPE + A_ROPE) ** -0.5 * LOG2E
    q = jnp.dot(cqn, wuq_ref[...], preferred_element_type=F32)
    for h in range(HA):
        c0 = A_QK * h
        qa = jnp.dot(q[:, c0:c0 + LANES].astype(BF16), wukt_ref[h],
                     preferred_element_type=F32)
        qa = (qa * scale).astype(BF16)
        qr = (rot(q[:, c0 + LANES:c0 + A_QK]) * scale).astype(BF16)
        for blk in range(tm // MLA_TQ):
            r0 = (blk * HA + h) * MLA_TQ
            q_ref[r0:r0 + MLA_TQ, :LANES] = qa[blk * MLA_TQ:(blk + 1) * MLA_TQ]
            q_ref[r0:r0 + MLA_TQ, LANES:] = qr[blk * MLA_TQ:(blk + 1) * MLA_TQ]
    k_ref[:, :LANES] = ckvn
    k_ref[:, LANES:] = rot(kr).astype(BF16)
    v_ref[:, :LANES] = ckvn
    v_ref[:, LANES:] = jnp.ones((tm, LANES), BF16)


def _mla_prep(lat, qn, kvn, wuq, wukt, rope, *, seq, tm=512):
    t = lat.shape[0]
    const = lambda i: (0, 0)
    return pl.pallas_call(
        _mla_prep_kernel,
        grid=(t // tm,),
        in_specs=[
            pl.BlockSpec((tm, LAT_W), lambda i: (i, 0)),
            pl.BlockSpec((1, Q_RANK), const),
            pl.BlockSpec((1, KV_RANK), const),
            pl.BlockSpec(wuq.shape, const),
            pl.BlockSpec(wukt.shape, lambda i: (0, 0, 0)),
            pl.BlockSpec((tm, 3 * LANES), lambda i: (i % (seq // tm), 0)),
        ],
        out_specs=[
            pl.BlockSpec((tm * HA, A_QK), lambda i: (i, 0)),
            pl.BlockSpec((tm, A_QK), lambda i: (i, 0)),
            pl.BlockSpec((tm, A_QK), lambda i: (i, 0)),
        ],
        out_shape=[
            jax.ShapeDtypeStruct((t * HA, A_QK), BF16),
            jax.ShapeDtypeStruct((t, A_QK), BF16),
            jax.ShapeDtypeStruct((t, A_QK), BF16),
        ],
        compiler_params=_params(1),
        name="mla_prep",
    )(lat, qn, kvn, wuq, wukt, rope)


def _flash_blocks(n, q_ref, k_ref, v_ref, sa_ref, sb_ref, m_ref, acc_ref, last_bias, exp_fn):
    tk = sa_ref.shape[1]

    def qk(t, dst_ref):
        off = pl.multiple_of(t * tk, tk)
        dst_ref[...] = lax.dot_general(q_ref[...], k_ref[pl.ds(off, tk), :], NT_DIMS,
                                       preferred_element_type=F32)

    def softmax_pv(t, src_ref, bias=None):
        s = src_ref[...]
        if bias is not None:
            s = s + bias
        m_prev = m_ref[...]
        m_next = jnp.maximum(m_prev, jnp.max(s, axis=1)[:, None])
        p = exp_fn(s - jnp.tile(m_next, (1, tk // LANES)))
        alpha = exp_fn(m_prev - m_next)
        off = pl.multiple_of(t * tk, tk)
        pv = jnp.dot(p.astype(BF16), v_ref[pl.ds(off, tk), :], preferred_element_type=F32)
        acc_ref[...] = jnp.tile(alpha, (1, 2)) * acc_ref[...] + pv
        m_ref[...] = m_next

    m_ref[...] = jnp.full(m_ref.shape, NEG, F32)
    acc_ref[...] = jnp.zeros(acc_ref.shape, F32)
    qk(0, sa_ref)

    def pair(t):
        qk(t + 1, sb_ref)
        softmax_pv(t, sa_ref)
        qk(t + 2, sa_ref)
        softmax_pv(t + 1, sb_ref)

    def quad(j, carry):
        pair(4 * j)
        pair(4 * j + 2)
        return carry

    lax.fori_loop(0, n >> 2, quad, 0)

    @pl.when((n & 2) != 0)
    def _():
        pair(n & ~3)

    @pl.when((n & 1) == 0)
    def _():
        softmax_pv(n, sa_ref, last_bias())

    @pl.when((n & 1) == 1)
    def _():
        qk(n, sb_ref)
        softmax_pv(n - 1, sa_ref)
        softmax_pv(n, sb_ref, last_bias())


def _flash_scratch(rows):
    return [pltpu.VMEM((rows, FLASH_TK), F32), pltpu.VMEM((rows, FLASH_TK), F32),
            pltpu.VMEM((rows, LANES), F32), pltpu.VMEM((rows, 2 * LANES), F32)]


def _mla_attn_kernel(q_ref, k_ref, v_ref, wuv_ref, o_ref, sa_ref, sb_ref, m_ref, acc_ref):
    i = pl.program_id(1)
    n = (i * MLA_TQ) // FLASH_TK

    def last_bias():
        shape = (MLA_TQ, FLASH_TK)
        rows = i * MLA_TQ + lax.broadcasted_iota(jnp.int32, shape, 0)
        cols = n * FLASH_TK + lax.broadcasted_iota(jnp.int32, shape, 1)
        bias = jnp.where((cols >> CHUNK_SHIFT) <= (rows >> CHUNK_SHIFT), 0.0, NEG)
        return jnp.tile(bias, (HA, 1))

    _flash_blocks(n, q_ref, k_ref, v_ref, sa_ref, sb_ref, m_ref, acc_ref, last_bias, jnp.exp2)

    for h in range(HA):
        a = acc_ref[h * MLA_TQ:(h + 1) * MLA_TQ, :]
        o = (a[:, :LANES] / a[:, LANES:]).astype(BF16)
        o_ref[:, h * A_V:(h + 1) * A_V] = jnp.dot(
            o, wuv_ref[h], preferred_element_type=F32).astype(BF16)


def _mla_attn(q, k, v, wuv, *, batch, seq):
    nq = seq // MLA_TQ
    rows = HA * MLA_TQ
    return pl.pallas_call(
        _mla_attn_kernel,
        grid=(batch, nq),
        in_specs=[
            pl.BlockSpec((rows, A_QK), lambda b, i: (b * nq + i, 0)),
            pl.BlockSpec((seq, A_QK), lambda b, i: (b, 0)),
            pl.BlockSpec((seq, A_QK), lambda b, i: (b, 0)),
            pl.BlockSpec(wuv.shape, lambda b, i: (0, 0, 0)),
        ],
        out_specs=pl.BlockSpec((MLA_TQ, HA * A_V), lambda b, i: (b * nq + i, 0)),
        out_shape=jax.ShapeDtypeStruct((batch * seq, HA * A_V), BF16),
        scratch_shapes=_flash_scratch(rows),
        compiler_params=_params(2),
        name="mla_attn",
    )(q, k, v, wuv)


def _diff_attn_kernel(q_ref, k_ref, v_ref, lq1_ref, lk1_ref, lq2_ref, lk2_ref, g_ref, o_ref,
                      kaug_ref, vaug_ref, dbias_ref, qs_ref, sa_ref, sb_ref, m_ref, acc_ref,
                      *, lam_init):
    h = pl.program_id(1)
    i = pl.program_id(2)
    tq = q_ref.shape[0]
    lane = lax.broadcasted_iota(jnp.int32, (tq, LANES), 1)
    slope_bits = (127 - (8 // HB) * (h + 1)) << 23
    slope = lax.bitcast_convert_type(jnp.full((1, LANES), slope_bits, jnp.int32), F32)

    @pl.when(i == 0)
    def _():
        r = lax.broadcasted_iota(jnp.int32, (tq, tq), 0)
        c = lax.broadcasted_iota(jnp.int32, (tq, tq), 1)
        ahead = jnp.maximum(c - r, 0).astype(F32)
        bias = jnp.tile(slope, (1, tq // LANES)) * (-2.0 * ahead)
        dbias_ref[...] = jnp.where((c >> CHUNK_SHIFT) <= (r >> CHUNK_SHIFT), bias, NEG)

        def fill(c, carry):
            off = pl.multiple_of(c * tq, tq)
            pos = off + lax.broadcasted_iota(jnp.int32, (tq, LANES), 0)
            cols = jnp.where(lane == 0, pos >> CHUNK_SHIFT,
                             jnp.where(lane == 1, pos & (CHUNK - 1), 0))
            kaug_ref[pl.ds(off, tq), :LANES] = k_ref[pl.ds(off, tq), :]
            kaug_ref[pl.ds(off, tq), LANES:] = cols.astype(F32).astype(BF16)
            vaug_ref[pl.ds(off, tq), :LANES] = v_ref[pl.ds(off, tq), :]
            vaug_ref[pl.ds(off, tq), LANES:] = jnp.ones((tq, LANES), BF16)
            return carry

        lax.fori_loop(0, k_ref.shape[0] // tq, fill, 0)

    pos_cols = jnp.where(lane == 0, slope * CHUNK, jnp.where(lane == 1, slope, 0.0)).astype(BF16)
    q = q_ref[...].astype(F32) * (B_DH ** -0.5)
    qs_ref[:tq, :LANES] = jnp.where(lane < B_DH, q, 0.0).astype(BF16)
    qs_ref[tq:, :LANES] = jnp.where(lane >= B_DH, q, 0.0).astype(BF16)
    qs_ref[:tq, LANES:] = pos_cols
    qs_ref[tq:, LANES:] = pos_cols

    def last_bias():
        return jnp.tile(dbias_ref[...], (2, 1))

    _flash_blocks(i, qs_ref, kaug_ref, vaug_ref, sa_ref, sb_ref, m_ref, acc_ref,
                  last_bias, jnp.exp)

    lam = (jnp.exp(jnp.sum(lq1_ref[...] * lk1_ref[...], axis=-1, keepdims=True))
           - jnp.exp(jnp.sum(lq2_ref[...] * lk2_ref[...], axis=-1, keepdims=True))
           + lam_init)
    o1 = acc_ref[:tq, :LANES] / acc_ref[:tq, LANES:]
    o2 = acc_ref[tq:, :LANES] / acc_ref[tq:, LANES:]
    o = o1 - lam * o2
    o_ref[...] = (_rms(o, g_ref[...]) * (1.0 - lam_init)).astype(BF16)


def _diff_attn(proj, lq1, lk1, lq2, lk2, subln, *, batch, seq, lam_init):
    tq = FLASH_TK
    nq = seq // tq
    vec = pl.BlockSpec((1, B_DH), lambda b, h, i: (0, 0))
    return pl.pallas_call(
        functools.partial(_diff_attn_kernel, lam_init=lam_init),
        grid=(batch, HB, nq),
        in_specs=[
            pl.BlockSpec((tq, LANES), lambda b, h, i: (b * nq + i, h)),
            pl.BlockSpec((seq, LANES), lambda b, h, i: (b, HB + h)),
            pl.BlockSpec((seq, LANES), lambda b, h, i: (b, 2 * HB + h)),
            vec, vec, vec, vec,
            pl.BlockSpec((1, 2 * B_DH), lambda b, h, i: (0, 0)),
        ],
        out_specs=pl.BlockSpec((tq, LANES), lambda b, h, i: (b * nq + i, h)),
        out_shape=jax.ShapeDtypeStruct((batch * seq, HB * 2 * B_DH), BF16),
        scratch_shapes=[pltpu.VMEM((seq, 2 * LANES), BF16), pltpu.VMEM((seq, 2 * LANES), BF16),
                        pltpu.VMEM((tq, tq), F32),
                        pltpu.VMEM((2 * tq, 2 * LANES), BF16)] + _flash_scratch(2 * tq),
        compiler_params=_params(3),
        name="diff_attn",
    )(proj, proj, proj, lq1, lk1, lq2, lk2, subln)


def _band_attn_kernel(q_ref, k_ref, v_ref, base_ref, o_ref, bias_ref, *, gq):
    b = pl.program_id(1)
    g2 = pl.program_id(2)
    win = 2 * gq

    @pl.when(jnp.logical_and(b == 0, g2 == 0))
    def _():
        x = jnp.broadcast_to(base_ref[0], (gq, win))
        x = pltpu.roll(x, 0, 1, stride=1, stride_axis=0)
        d = _chunk_ids((gq, win), 1) - _chunk_ids((gq, win), 0)
        bias_ref[...] = jnp.where(jnp.logical_and(d >= 0, d <= LEFT_CHUNKS), x * LOG2E, NEG)

    def attend(half, k, v, bias):
        rows = slice(half * gq, (half + 1) * gq)
        q = (q_ref[rows, :].astype(F32) * (C_DH ** -0.5 * LOG2E)).astype(BF16)
        s = lax.dot_general(q, k, NT_DIMS, preferred_element_type=F32) + bias
        p = jnp.exp2(s - jnp.max(s, axis=-1, keepdims=True))
        l = jnp.sum(p, axis=-1, keepdims=True)
        o = jnp.dot(p.astype(BF16), v, preferred_element_type=F32)
        o_ref[rows, :] = (o / l).astype(BF16)

    @pl.when(g2 == 0)
    def _():
        attend(0, k_ref[:gq, :], v_ref[:gq, :], bias_ref[:, gq:])
        attend(1, k_ref[:win, :], v_ref[:win, :], bias_ref[...])

    @pl.when(g2 > 0)
    def _():
        off0 = pl.multiple_of((2 * g2 - 1) * gq, gq)
        off1 = pl.multiple_of(2 * g2 * gq, gq)
        attend(0, k_ref[pl.ds(off0, win), :], v_ref[pl.ds(off0, win), :], bias_ref[...])
        attend(1, k_ref[pl.ds(off1, win), :], v_ref[pl.ds(off1, win), :], bias_ref[...])


def _band_attn(proj, base, *, batch, seq):
    gq = LEFT_CHUNKS * CHUNK
    ng = seq // (2 * gq)
    qc, kc, vc = 3 * HB, 3 * HB + HC, 3 * HB + 2 * HC
    return pl.pallas_call(
        functools.partial(_band_attn_kernel, gq=gq),
        grid=(HC, batch, ng),
        in_specs=[
            pl.BlockSpec((2 * gq, C_DH), lambda h, b, g: (b * ng + g, qc + h)),
            pl.BlockSpec((seq, C_DH), lambda h, b, g: (b, kc + h)),
            pl.BlockSpec((seq, C_DH), lambda h, b, g: (b, vc + h)),
            pl.BlockSpec((1, 1, 2 * gq), lambda h, b, g: (h, 0, 0)),
        ],
        out_specs=pl.BlockSpec((2 * gq, C_DH), lambda h, b, g: (b * ng + g, h)),
        out_shape=jax.ShapeDtypeStruct((batch * seq, HC * C_DH), BF16),
        scratch_shapes=[pltpu.VMEM((gq, 2 * gq), F32)],
        compiler_params=_params(3),
        name="band_attn",
    )(proj, proj, proj, base)


def _oproj_kernel(a_ref, b_ref, c_ref, x_ref, w_ref, o_ref):
    na, nb = a_ref.shape[1], a_ref.shape[1] + b_ref.shape[1]
    acc = jnp.dot(a_ref[...], w_ref[:na, :], preferred_element_type=F32)
    acc += jnp.dot(b_ref[...], w_ref[na:nb, :], preferred_element_type=F32)
    acc += jnp.dot(c_ref[...], w_ref[nb:, :], preferred_element_type=F32)
    o_ref[...] = x_ref[...] + acc


def _oproj(a, b, c, x, w, *, tm=512):
    t, d = x.shape
    row = lambda i: (i, 0)
    return pl.pallas_call(
        _oproj_kernel,
        grid=(t // tm,),
        in_specs=[
            pl.BlockSpec((tm, a.shape[1]), row),
            pl.BlockSpec((tm, b.shape[1]), row),
            pl.BlockSpec((tm, c.shape[1]), row),
            pl.BlockSpec((tm, d), row),
            pl.BlockSpec(w.shape, lambda i: (0, 0)),
        ],
        out_specs=pl.BlockSpec((tm, d), row),
        out_shape=jax.ShapeDtypeStruct((t, d), F32),
        compiler_params=_params(1),
        name="oproj",
    )(a, b, c, x, w)


def _mlp_kernel(x_ref, g_ref, wu_ref, wd_ref, gf_ref, o_ref, xn_ref, *, rows, final_norm):
    f = pl.program_id(1)

    @pl.when(f == 0)
    def _():
        for r in range(0, x_ref.shape[0], rows):
            xr = x_ref[r:r + rows, :]
            xn_ref[r:r + rows, :] = _rms(xr, g_ref[...]).astype(BF16)
            o_ref[r:r + rows, :] = xr

    hidden = jnp.dot(xn_ref[...], wu_ref[...], preferred_element_type=F32)
    hidden = jnp.square(jnp.maximum(hidden, 0.0)).astype(BF16)
    o_ref[...] += jnp.dot(hidden, wd_ref[...], preferred_element_type=F32)

    if final_norm:
        @pl.when(f == pl.num_programs(1) - 1)
        def _():
            for r in range(0, x_ref.shape[0], rows):
                o_ref[r:r + rows, :] = _rms(o_ref[r:r + rows, :], gf_ref[...])


def _mlp(x, g, wu, wd, gf, *, final_norm, tm=1024, tf=512):
    t, d = x.shape
    dff = wu.shape[1]
    return pl.pallas_call(
        functools.partial(_mlp_kernel, rows=256, final_norm=final_norm),
        grid=(t // tm, dff // tf),
        in_specs=[
            pl.BlockSpec((tm, d), lambda i, f: (i, 0)),
            pl.BlockSpec((1, d), lambda i, f: (0, 0)),
            pl.BlockSpec((d, tf), lambda i, f: (0, f)),
            pl.BlockSpec((tf, d), lambda i, f: (f, 0)),
            pl.BlockSpec((1, d), lambda i, f: (0, 0)),
        ],
        out_specs=pl.BlockSpec((tm, d), lambda i, f: (i, 0)),
        out_shape=jax.ShapeDtypeStruct((t, d), F32),
        scratch_shapes=[pltpu.VMEM((tm, d), BF16)],
        compiler_params=_params(2),
        name="mlp",
    )(x, g, wu, wd, gf)


def _rope_tables(seq):
    half = A_ROPE // 2
    inv_freq = ROPE_THETA ** (-jnp.arange(half, dtype=F32) / half)
    ang = jnp.arange(seq, dtype=F32)[:, None] * inv_freq[None, :]
    cos, sin = jnp.cos(ang), jnp.sin(ang)
    z = jnp.zeros_like(cos)
    pad = jnp.zeros((seq, LANES - A_ROPE), F32)
    return jnp.concatenate([cos, cos, pad, z, sin, pad, -sin, z, pad], axis=1)


def _split_in_weight(w_in):
    lat = Q_RANK + KV_RANK + A_ROPE
    w_lat = jnp.pad(w_in[:, :lat], ((0, 0), (0, LAT_W - lat))).astype(BF16)
    return w_lat, w_in[:, lat:].astype(BF16)


def _band_base(rel_bias):
    gq = LEFT_CHUNKS * CHUNK
    edge = rel_bias[:, REL_SIZE - 1:]
    head = jnp.broadcast_to(edge, (HC, gq - REL_CLIP))
    tail = jnp.broadcast_to(edge, (HC, 2 * gq - (gq - REL_CLIP) - REL_SIZE))
    return jnp.concatenate([head, rel_bias[:, ::-1], tail], axis=1).reshape(HC, 1, 2 * gq)


def kernel(x, attn_norm, w_in, q_a_norm, kv_a_norm, w_uq, w_ukv, lambda_q1, lambda_k1,
           lambda_q2, lambda_k2, diff_subln, rel_bias, w_o, mlp_norm, w_up, w_down, final_norm):
    batch, seq, d = x.shape
    xf = x.reshape(batch * seq, d)
    rope = _rope_tables(seq)
    row = lambda v: v.reshape(1, -1)
    for l in range(DEPTH):
        lat, proj = _inproj(xf, row(attn_norm[l]), *_split_in_weight(w_in[l]))

        wuq = w_uq[l].reshape(Q_RANK, HA, A_NOPE + A_ROPE)
        wuq = jnp.pad(wuq, ((0, 0), (0, 0), (0, A_QK - A_NOPE - A_ROPE)))
        wuq = wuq.reshape(Q_RANK, HA * A_QK).astype(BF16)
        wukv = w_ukv[l].reshape(KV_RANK, HA, A_NOPE + A_V)
        wukt = wukv[:, :, :A_NOPE].transpose(1, 2, 0).astype(BF16)
        wuv = wukv[:, :, A_NOPE:].transpose(1, 0, 2).astype(BF16)
        q, k, v = _mla_prep(lat, row(q_a_norm[l]), row(kv_a_norm[l]), wuq, wukt, rope, seq=seq)
        out_a = _mla_attn(q, k, v, wuv, batch=batch, seq=seq)

        lam_init = 0.8 - 0.6 * math.exp(-0.3 * l)
        out_b = _diff_attn(proj, row(lambda_q1[l]), row(lambda_k1[l]), row(lambda_q2[l]),
                           row(lambda_k2[l]), row(diff_subln[l]),
                           batch=batch, seq=seq, lam_init=lam_init)
        out_c = _band_attn(proj, _band_base(rel_bias[l]), batch=batch, seq=seq)

        xf = _oproj(out_a, out_b, out_c, xf, w_o[l].astype(BF16))
        xf = _mlp(xf, row(mlp_norm[l]), w_up[l].astype(BF16), w_down[l].astype(BF16),
                  row(final_norm), final_norm=(l == DEPTH - 1))
    return xf.reshape(batch, seq, d)
```

```python
import functools
import math

import jax
import jax.numpy as jnp
from jax import lax
from jax.experimental import pallas as pl
from jax.experimental.pallas import tpu as pltpu

F32 = jnp.float32
BF16 = jnp.bfloat16

D_MODEL = 2048
DEPTH = 2
CHUNK = 64
CHUNK_SHIFT = CHUNK.bit_length() - 1
HA, A_NOPE, A_ROPE, A_V = 8, 128, 64, 128
Q_RANK, KV_RANK = 384, 128
ROPE_THETA = 10000.0
HB, B_DH = 4, 64
HC, C_DH = 4, 128
LEFT_CHUNKS = 8
REL_CLIP = 256
REL_SIZE = REL_CLIP + CHUNK
D_FF = 4 * D_MODEL
EPS = 1e-6
NEG = -1e30

LANES = 128
LAT_W = 768
PROJ_W = 6 * 512
A_QK = 256
MLA_TQ = 256
FLASH_TK = 512
LOG2E = math.log2(math.e)
VMEM_LIMIT = 56 * 1024 * 1024

NT_DIMS = (((1,), (1,)), ((), ()))


def _params(n_axes, vmem=VMEM_LIMIT):
    return pltpu.CompilerParams(
        dimension_semantics=("arbitrary",) * n_axes, vmem_limit_bytes=vmem)


def _rms(x, g):
    return x * lax.rsqrt(jnp.mean(x * x, axis=-1, keepdims=True) + EPS) * g


def _chunk_ids(shape, dim):
    return lax.broadcasted_iota(jnp.int32, shape, dim) >> CHUNK_SHIFT


def _inproj_kernel(x_ref, g_ref, wl_ref, wp_ref, lat_ref, proj_ref, xn_ref, *, rows):
    j = pl.program_id(1)

    @pl.when(j == 0)
    def _():
        for r in range(0, x_ref.shape[0], rows):
            xn_ref[r:r + rows, :] = _rms(x_ref[r:r + rows, :], g_ref[...]).astype(BF16)
        lat_ref[...] = jnp.dot(xn_ref[...], wl_ref[...], preferred_element_type=F32)

    @pl.when(j > 0)
    def _():
        proj_ref[...] = jnp.dot(xn_ref[...], wp_ref[...],
                                preferred_element_type=F32).astype(BF16)


def _inproj(x, g, w_lat, w_proj, *, tm=1024):
    t, d = x.shape
    n_tiles = 1 + w_proj.shape[1] // LAT_W
    return pl.pallas_call(
        functools.partial(_inproj_kernel, rows=256),
        grid=(t // tm, n_tiles),
        in_specs=[
            pl.BlockSpec((tm, d), lambda i, j: (i, 0)),
            pl.BlockSpec((1, d), lambda i, j: (0, 0)),
            pl.BlockSpec((d, LAT_W), lambda i, j: (0, 0)),
            pl.BlockSpec((d, LAT_W), lambda i, j: (0, jnp.maximum(j - 1, 0))),
        ],
        out_specs=[
            pl.BlockSpec((tm, LAT_W), lambda i, j: (i, 0)),
            pl.BlockSpec((tm, LAT_W), lambda i, j: (i, jnp.maximum(j - 1, 0))),
        ],
        out_shape=[
            jax.ShapeDtypeStruct((t, LAT_W), F32),
            jax.ShapeDtypeStruct((t, PROJ_W), BF16),
        ],
        scratch_shapes=[pltpu.VMEM((tm, d), BF16)],
        compiler_params=_params(2),
        name="inproj",
    )(x, g, w_lat, w_proj)


def _mla_prep_kernel(lat_ref, qn_ref, kvn_ref, wuq_ref, wukt_ref, rope_ref,
                     q_ref, k_ref, v_ref):
    tm = lat_ref.shape[0]
    cqn = _rms(lat_ref[:, :Q_RANK], qn_ref[...]).astype(BF16)
    ckvn = _rms(lat_ref[:, Q_RANK:Q_RANK + KV_RANK], kvn_ref[...]).astype(BF16)
    kr = lat_ref[:, Q_RANK + KV_RANK:Q_RANK + KV_RANK + LANES]
    cos = rope_ref[:, :LANES]
    sin_hi = rope_ref[:, LANES:2 * LANES]
    sin_lo = rope_ref[:, 2 * LANES:]

    def rot(r):
        return (r * cos + pltpu.roll(r, A_ROPE // 2, 1) * sin_hi
                + pltpu.roll(r, LANES - A_ROPE // 2, 1) * sin_lo)

    scale = (A_NOPE + A_ROPE) ** -0.5 * LOG2E
    q = jnp.dot(cqn, wuq_ref[...], preferred_element_type=F32)
    for h in range(HA):
        c0 = A_QK * h
        qa = jnp.dot(q[:, c0:c0 + LANES].astype(BF16), wukt_ref[h],
                     preferred_element_type=F32)
        qa = (qa * scale).astype(BF16)
        qr = (rot(q[:, c0 + LANES:c0 + A_QK]) * scale).astype(BF16)
        for blk in range(tm // MLA_TQ):
            r0 = (blk * HA + h) * MLA_TQ
            q_ref[r0:r0 + MLA_TQ, :LANES] = qa[blk * MLA_TQ:(blk + 1) * MLA_TQ]
            q_ref[r0:r0 + MLA_TQ, LANES:] = qr[blk * MLA_TQ:(blk + 1) * MLA_TQ]
    k_ref[:, :LANES] = ckvn
    k_ref[:, LANES:] = rot(kr).astype(BF16)
    v_ref[:, :LANES] = ckvn
    v_ref[:, LANES:] = jnp.ones((tm, LANES), BF16)


def _mla_prep(lat, qn, kvn, wuq, wukt, rope, *, seq, tm=512):
    t = lat.shape[0]
    const = lambda i: (0, 0)
    return pl.pallas_call(
        _mla_prep_kernel,
        grid=(t // tm,),
        in_specs=[
            pl.BlockSpec((tm, LAT_W), lambda i: (i, 0)),
            pl.BlockSpec((1, Q_RANK), const),
            pl.BlockSpec((1, KV_RANK), const),
            pl.BlockSpec(wuq.shape, const),
            pl.BlockSpec(wukt.shape, lambda i: (0, 0, 0)),
            pl.BlockSpec((tm, 3 * LANES), lambda i: (i % (seq // tm), 0)),
        ],
        out_specs=[
            pl.BlockSpec((tm * HA, A_QK), lambda i: (i, 0)),
            pl.BlockSpec((tm, A_QK), lambda i: (i, 0)),
            pl.BlockSpec((tm, A_QK), lambda i: (i, 0)),
        ],
        out_shape=[
            jax.ShapeDtypeStruct((t * HA, A_QK), BF16),
            jax.ShapeDtypeStruct((t, A_QK), BF16),
            jax.ShapeDtypeStruct((t, A_QK), BF16),
        ],
        compiler_params=_params(1),
        name="mla_prep",
    )(lat, qn, kvn, wuq, wukt, rope)


def _flash_ops(q_ref, k_ref, v_ref, sa_ref, sb_ref, m_ref, acc_ref, exp_fn):
    tk = sa_ref.shape[1]
    everything = slice(None)

    def qk(t, dst_ref, rows=everything):
        off = pl.multiple_of(t * tk, tk)
        dst_ref[rows, :] = lax.dot_general(q_ref[rows, :], k_ref[pl.ds(off, tk), :], NT_DIMS,
                                           preferred_element_type=F32)

    def softmax_pv(t, src_ref, bias=None, rows=everything):
        s = src_ref[rows, :]
        if bias is not None:
            nb = bias.shape[0]
            s = jnp.concatenate([s[:nb] + bias, s[nb:]], axis=0) if nb < s.shape[0] else s + bias
        m_prev = m_ref[rows, :]
        m_next = jnp.maximum(m_prev, jnp.max(s, axis=1)[:, None])
        p = exp_fn(s - jnp.tile(m_next, (1, tk // LANES)))
        alpha = exp_fn(m_prev - m_next)
        off = pl.multiple_of(t * tk, tk)
        pv = jnp.dot(p.astype(BF16), v_ref[pl.ds(off, tk), :], preferred_element_type=F32)
        acc_ref[rows, :] = jnp.tile(alpha, (1, 2)) * acc_ref[rows, :] + pv
        m_ref[rows, :] = m_next

    def pair(t):
        qk(t + 1, sb_ref)
        softmax_pv(t, sa_ref)
        qk(t + 2, sa_ref)
        softmax_pv(t + 1, sb_ref)

    def start_and_full_pairs(npairs):
        m_ref[...] = jnp.full(m_ref.shape, NEG, F32)
        acc_ref[...] = jnp.zeros(acc_ref.shape, F32)
        qk(0, sa_ref)

        def quad(j, carry):
            pair(4 * j)
            pair(4 * j + 2)
            return carry

        lax.fori_loop(0, npairs >> 1, quad, 0)

        @pl.when((npairs & 1) != 0)
        def _():
            pair(2 * (npairs - 1))

    return qk, softmax_pv, start_and_full_pairs


def _flash_scratch(rows):
    return [pltpu.VMEM((rows, FLASH_TK), F32), pltpu.VMEM((rows, FLASH_TK), F32),
            pltpu.VMEM((rows, LANES), F32), pltpu.VMEM((rows, 2 * LANES), F32)]


def _mla_attn_kernel(q_ref, k_ref, v_ref, wuv_ref, o_ref, sa_ref, sb_ref, m_ref, acc_ref):
    i = pl.program_id(1)
    n = (i * MLA_TQ) // FLASH_TK

    def last_bias():
        shape = (MLA_TQ, FLASH_TK)
        rows = i * MLA_TQ + lax.broadcasted_iota(jnp.int32, shape, 0)
        cols = n * FLASH_TK + lax.broadcasted_iota(jnp.int32, shape, 1)
        bias = jnp.where((cols >> CHUNK_SHIFT) <= (rows >> CHUNK_SHIFT), 0.0, NEG)
        return jnp.tile(bias, (HA, 1))

    qk, softmax_pv, start_and_full_pairs = _flash_ops(
        q_ref, k_ref, v_ref, sa_ref, sb_ref, m_ref, acc_ref, jnp.exp2)
    start_and_full_pairs(n >> 1)

    @pl.when((n & 1) == 0)
    def _():
        softmax_pv(n, sa_ref, last_bias())

    @pl.when((n & 1) == 1)
    def _():
        qk(n, sb_ref)
        softmax_pv(n - 1, sa_ref)
        softmax_pv(n, sb_ref, last_bias())

    for h in range(HA):
        a = acc_ref[h * MLA_TQ:(h + 1) * MLA_TQ, :]
        o = (a[:, :LANES] / a[:, LANES:]).astype(BF16)
        o_ref[:, h * A_V:(h + 1) * A_V] = jnp.dot(
            o, wuv_ref[h], preferred_element_type=F32).astype(BF16)


def _mla_attn(q, k, v, wuv, *, batch, seq):
    nq = seq // MLA_TQ
    rows = HA * MLA_TQ
    return pl.pallas_call(
        _mla_attn_kernel,
        grid=(batch, nq),
        in_specs=[
            pl.BlockSpec((rows, A_QK), lambda b, i: (b * nq + i, 0)),
            pl.BlockSpec((seq, A_QK), lambda b, i: (b, 0)),
            pl.BlockSpec((seq, A_QK), lambda b, i: (b, 0)),
            pl.BlockSpec(wuv.shape, lambda b, i: (0, 0, 0)),
        ],
        out_specs=pl.BlockSpec((MLA_TQ, HA * A_V), lambda b, i: (b * nq + i, 0)),
        out_shape=jax.ShapeDtypeStruct((batch * seq, HA * A_V), BF16),
        scratch_shapes=_flash_scratch(rows),
        compiler_params=_params(2),
        name="mla_attn",
    )(q, k, v, wuv)


def _diff_attn_kernel(q_ref, k_ref, v_ref, lq1_ref, lk1_ref, lq2_ref, lk2_ref, g_ref, o_ref,
                      kaug_ref, vaug_ref, dbias_ref, qs_ref, sa_ref, sb_ref, m_ref, acc_ref,
                      *, lam_init):
    h = pl.program_id(1)
    i = pl.program_id(2)
    tq = FLASH_TK
    lane = lax.broadcasted_iota(jnp.int32, (tq, LANES), 1)
    slope_bits = (127 - (8 // HB) * (h + 1)) << 23
    slope = lax.bitcast_convert_type(jnp.full((1, LANES), slope_bits, jnp.int32), F32)

    @pl.when(i == 0)
    def _():
        r = lax.broadcasted_iota(jnp.int32, (tq, tq), 0)
        c = lax.broadcasted_iota(jnp.int32, (tq, tq), 1)
        ahead = jnp.maximum(c - r, 0).astype(F32)
        bias = jnp.tile(slope, (1, tq // LANES)) * (-2.0 * ahead)
        dbias_ref[...] = jnp.where((c >> CHUNK_SHIFT) <= (r >> CHUNK_SHIFT), bias, NEG)

        def fill(c, carry):
            off = pl.multiple_of(c * tq, tq)
            pos = off + lax.broadcasted_iota(jnp.int32, (tq, LANES), 0)
            cols = jnp.where(lane == 0, pos >> CHUNK_SHIFT,
                             jnp.where(lane == 1, pos & (CHUNK - 1), 0))
            kaug_ref[pl.ds(off, tq), :LANES] = k_ref[pl.ds(off, tq), :]
            kaug_ref[pl.ds(off, tq), LANES:] = cols.astype(F32).astype(BF16)
            vaug_ref[pl.ds(off, tq), :LANES] = v_ref[pl.ds(off, tq), :]
            vaug_ref[pl.ds(off, tq), LANES:] = jnp.ones((tq, LANES), BF16)
            return carry

        lax.fori_loop(0, k_ref.shape[0] // tq, fill, 0)

    pos_cols = jnp.where(lane == 0, slope * CHUNK, jnp.where(lane == 1, slope, 0.0)).astype(BF16)
    for half in range(2):
        q = q_ref[half * tq:(half + 1) * tq, :].astype(F32) * (B_DH ** -0.5)
        r0 = 2 * half * tq
        qs_ref[r0:r0 + tq, :LANES] = jnp.where(lane < B_DH, q, 0.0).astype(BF16)
        qs_ref[r0 + tq:r0 + 2 * tq, :LANES] = jnp.where(lane >= B_DH, q, 0.0).astype(BF16)
        qs_ref[r0:r0 + tq, LANES:] = pos_cols
        qs_ref[r0 + tq:r0 + 2 * tq, LANES:] = pos_cols

    qk, softmax_pv, start_and_full_pairs = _flash_ops(
        qs_ref, kaug_ref, vaug_ref, sa_ref, sb_ref, m_ref, acc_ref, jnp.exp)
    start_and_full_pairs(i)
    bottom = slice(2 * tq, 4 * tq)
    diag_bias = jnp.tile(dbias_ref[...], (2, 1))
    qk(2 * i + 1, sb_ref, rows=bottom)
    softmax_pv(2 * i, sa_ref, diag_bias)
    softmax_pv(2 * i + 1, sb_ref, diag_bias, rows=bottom)

    lam = (jnp.exp(jnp.sum(lq1_ref[...] * lk1_ref[...], axis=-1, keepdims=True))
           - jnp.exp(jnp.sum(lq2_ref[...] * lk2_ref[...], axis=-1, keepdims=True))
           + lam_init)
    for half in range(2):
        r0 = 2 * half * tq
        o1 = acc_ref[r0:r0 + tq, :LANES] / acc_ref[r0:r0 + tq, LANES:]
        o2 = acc_ref[r0 + tq:r0 + 2 * tq, :LANES] / acc_ref[r0 + tq:r0 + 2 * tq, LANES:]
        o = o1 - lam * o2
        o_ref[half * tq:(half + 1) * tq, :] = (
            _rms(o, g_ref[...]) * (1.0 - lam_init)).astype(BF16)


def _diff_attn(proj, lq1, lk1, lq2, lk2, subln, *, batch, seq, lam_init):
    tq = 2 * FLASH_TK
    nq = seq // tq
    vec = pl.BlockSpec((1, B_DH), lambda b, h, i: (0, 0))
    return pl.pallas_call(
        functools.partial(_diff_attn_kernel, lam_init=lam_init),
        grid=(batch, HB, nq),
        in_specs=[
            pl.BlockSpec((tq, LANES), lambda b, h, i: (b * nq + i, h)),
            pl.BlockSpec((seq, LANES), lambda b, h, i: (b, HB + h)),
            pl.BlockSpec((seq, LANES), lambda b, h, i: (b, 2 * HB + h)),
            vec, vec, vec, vec,
            pl.BlockSpec((1, 2 * B_DH), lambda b, h, i: (0, 0)),
        ],
        out_specs=pl.BlockSpec((tq, LANES), lambda b, h, i: (b * nq + i, h)),
        out_shape=jax.ShapeDtypeStruct((batch * seq, HB * 2 * B_DH), BF16),
        scratch_shapes=[pltpu.VMEM((seq, 2 * LANES), BF16), pltpu.VMEM((seq, 2 * LANES), BF16),
                        pltpu.VMEM((FLASH_TK, FLASH_TK), F32),
                        pltpu.VMEM((2 * tq, 2 * LANES), BF16)] + _flash_scratch(2 * tq),
        compiler_params=_params(3),
        name="diff_attn",
    )(proj, proj, proj, lq1, lk1, lq2, lk2, subln)


def _band_attn_kernel(q_ref, k_ref, v_ref, base_ref, o_ref, bias_ref, *, gq):
    b = pl.program_id(1)
    g2 = pl.program_id(2)
    win = 2 * gq

    @pl.when(jnp.logical_and(b == 0, g2 == 0))
    def _():
        x = jnp.broadcast_to(base_ref[0], (gq, win))
        x = pltpu.roll(x, 0, 1, stride=1, stride_axis=0)
        d = _chunk_ids((gq, win), 1) - _chunk_ids((gq, win), 0)
        bias_ref[...] = jnp.where(jnp.logical_and(d >= 0, d <= LEFT_CHUNKS), x * LOG2E, NEG)

    def attend(half, k, v, bias):
        rows = slice(half * gq, (half + 1) * gq)
        q = (q_ref[rows, :].astype(F32) * (C_DH ** -0.5 * LOG2E)).astype(BF16)
        s = lax.dot_general(q, k, NT_DIMS, preferred_element_type=F32) + bias
        p = jnp.exp2(s - jnp.max(s, axis=-1, keepdims=True))
        l = jnp.sum(p, axis=-1, keepdims=True)
        o = jnp.dot(p.astype(BF16), v, preferred_element_type=F32)
        o_ref[rows, :] = (o / l).astype(BF16)

    @pl.when(g2 == 0)
    def _():
        attend(0, k_ref[:gq, :], v_ref[:gq, :], bias_ref[:, gq:])
        attend(1, k_ref[:win, :], v_ref[:win, :], bias_ref[...])

    @pl.when(g2 > 0)
    def _():
        off0 = pl.multiple_of((2 * g2 - 1) * gq, gq)
        off1 = pl.multiple_of(2 * g2 * gq, gq)
        attend(0, k_ref[pl.ds(off0, win), :], v_ref[pl.ds(off0, win), :], bias_ref[...])
        attend(1, k_ref[pl.ds(off1, win), :], v_ref[pl.ds(off1, win), :], bias_ref[...])


def _band_attn(proj, base, *, batch, seq):
    gq = LEFT_CHUNKS * CHUNK
    ng = seq // (2 * gq)
    qc, kc, vc = 3 * HB, 3 * HB + HC, 3 * HB + 2 * HC
    return pl.pallas_call(
        functools.partial(_band_attn_kernel, gq=gq),
        grid=(HC, batch, ng),
        in_specs=[
            pl.BlockSpec((2 * gq, C_DH), lambda h, b, g: (b * ng + g, qc + h)),
            pl.BlockSpec((seq, C_DH), lambda h, b, g: (b, kc + h)),
            pl.BlockSpec((seq, C_DH), lambda h, b, g: (b, vc + h)),
            pl.BlockSpec((1, 1, 2 * gq), lambda h, b, g: (h, 0, 0)),
        ],
        out_specs=pl.BlockSpec((2 * gq, C_DH), lambda h, b, g: (b * ng + g, h)),
        out_shape=jax.ShapeDtypeStruct((batch * seq, HC * C_DH), BF16),
        scratch_shapes=[pltpu.VMEM((gq, 2 * gq), F32)],
        compiler_params=_params(3),
        name="band_attn",
    )(proj, proj, proj, base)


def _oproj_kernel(a_ref, b_ref, c_ref, x_ref, w_ref, o_ref):
    na, nb = a_ref.shape[1], a_ref.shape[1] + b_ref.shape[1]
    acc = jnp.dot(a_ref[...], w_ref[:na, :], preferred_element_type=F32)
    acc += jnp.dot(b_ref[...], w_ref[na:nb, :], preferred_element_type=F32)
    acc += jnp.dot(c_ref[...], w_ref[nb:, :], preferred_element_type=F32)
    o_ref[...] = x_ref[...] + acc


def _oproj(a, b, c, x, w, layer, *, tm=512):
    t, d = x.shape
    row = lambda i: (i, 0)
    return pl.pallas_call(
        _oproj_kernel,
        grid=(t // tm,),
        in_specs=[
            pl.BlockSpec((tm, a.shape[1]), row),
            pl.BlockSpec((tm, b.shape[1]), row),
            pl.BlockSpec((tm, c.shape[1]), row),
            pl.BlockSpec((tm, d), row),
            pl.BlockSpec((None,) + w.shape[1:], lambda i: (layer, 0, 0)),
        ],
        out_specs=pl.BlockSpec((tm, d), row),
        out_shape=jax.ShapeDtypeStruct((t, d), F32),
        compiler_params=_params(1),
        name="oproj",
    )(a, b, c, x, w)


def _mlp_kernel(x_ref, g_ref, wu_ref, wd_ref, gf_ref, o_ref, xn_ref, *, rows, final_norm):
    f = pl.program_id(1)

    @pl.when(f == 0)
    def _():
        for r in range(0, x_ref.shape[0], rows):
            xr = x_ref[r:r + rows, :]
            xn_ref[r:r + rows, :] = _rms(xr, g_ref[...]).astype(BF16)
            o_ref[r:r + rows, :] = xr

    hidden = jnp.dot(xn_ref[...], wu_ref[...], preferred_element_type=F32)
    hidden = jnp.square(jnp.maximum(hidden, 0.0)).astype(BF16)
    o_ref[...] += jnp.dot(hidden, wd_ref[...], preferred_element_type=F32)

    if final_norm:
        @pl.when(f == pl.num_programs(1) - 1)
        def _():
            for r in range(0, x_ref.shape[0], rows):
                o_ref[r:r + rows, :] = _rms(o_ref[r:r + rows, :], gf_ref[...])


def _mlp(x, g, wu, wd, gf, layer, *, final_norm, tm=1024, tf=512):
    t, d = x.shape
    dff = wu.shape[2]
    return pl.pallas_call(
        functools.partial(_mlp_kernel, rows=256, final_norm=final_norm),
        grid=(t // tm, dff // tf),
        in_specs=[
            pl.BlockSpec((tm, d), lambda i, f: (i, 0)),
            pl.BlockSpec((1, d), lambda i, f: (0, 0)),
            pl.BlockSpec((None, d, tf), lambda i, f: (layer, 0, f)),
            pl.BlockSpec((None, tf, d), lambda i, f: (layer, f, 0)),
            pl.BlockSpec((1, d), lambda i, f: (0, 0)),
        ],
        out_specs=pl.BlockSpec((tm, d), lambda i, f: (i, 0)),
        out_shape=jax.ShapeDtypeStruct((t, d), F32),
        scratch_shapes=[pltpu.VMEM((tm, d), BF16)],
        compiler_params=_params(2),
        name="mlp",
    )(x, g, wu, wd, gf)


def _rope_tables(seq):
    half = A_ROPE // 2
    lane = jnp.arange(LANES)
    inv_freq = ROPE_THETA ** (-(lane % half).astype(F32) / half)
    ang = jnp.arange(seq, dtype=F32)[:, None] * inv_freq[None, :]
    cos, sin = jnp.cos(ang), jnp.sin(ang)
    return jnp.concatenate([
        jnp.where(lane < A_ROPE, cos, 0.0),
        jnp.where((lane >= half) & (lane < A_ROPE), sin, 0.0),
        jnp.where(lane < half, -sin, 0.0)], axis=1)


def _split_in_weight(w_in):
    lat = Q_RANK + KV_RANK + A_ROPE
    w_lat = jnp.pad(w_in[:, :lat], ((0, 0), (0, LAT_W - lat))).astype(BF16)
    return w_lat, w_in[:, lat:].astype(BF16)


def _band_base(rel_bias):
    gq = LEFT_CHUNKS * CHUNK
    edge = rel_bias[:, REL_SIZE - 1:]
    head = jnp.broadcast_to(edge, (HC, gq - REL_CLIP))
    tail = jnp.broadcast_to(edge, (HC, 2 * gq - (gq - REL_CLIP) - REL_SIZE))
    return jnp.concatenate([head, rel_bias[:, ::-1], tail], axis=1).reshape(HC, 1, 2 * gq)


def kernel(x, attn_norm, w_in, q_a_norm, kv_a_norm, w_uq, w_ukv, lambda_q1, lambda_k1,
           lambda_q2, lambda_k2, diff_subln, rel_bias, w_o, mlp_norm, w_up, w_down, final_norm):
    batch, seq, d = x.shape
    xf = x.reshape(batch * seq, d)
    rope = _rope_tables(seq)
    row = lambda v: v.reshape(1, -1)
    wo_all, wu_all, wd_all = w_o.astype(BF16), w_up.astype(BF16), w_down.astype(BF16)
    for l in range(DEPTH):
        lat, proj = _inproj(xf, row(attn_norm[l]), *_split_in_weight(w_in[l]))

        wuq = w_uq[l].reshape(Q_RANK, HA, A_NOPE + A_ROPE)
        wuq = jnp.pad(wuq, ((0, 0), (0, 0), (0, A_QK - A_NOPE - A_ROPE)))
        wuq = wuq.reshape(Q_RANK, HA * A_QK).astype(BF16)
        wukv = w_ukv[l].reshape(KV_RANK, HA, A_NOPE + A_V)
        wukt = wukv[:, :, :A_NOPE].transpose(1, 2, 0).astype(BF16)
        wuv = wukv[:, :, A_NOPE:].transpose(1, 0, 2).astype(BF16)
        q, k, v = _mla_prep(lat, row(q_a_norm[l]), row(kv_a_norm[l]), wuq, wukt, rope, seq=seq)
        out_a = _mla_attn(q, k, v, wuv, batch=batch, seq=seq)

        lam_init = 0.8 - 0.6 * math.exp(-0.3 * l)
        out_b = _diff_attn(proj, row(lambda_q1[l]), row(lambda_k1[l]), row(lambda_q2[l]),
                           row(lambda_k2[l]), row(diff_subln[l]),
                           batch=batch, seq=seq, lam_init=lam_init)
        out_c = _band_attn(proj, _band_base(rel_bias[l]), batch=batch, seq=seq)

        xf = _oproj(out_a, out_b, out_c, xf, wo_all, l)
        xf = _mlp(xf, row(mlp_norm[l]), wu_all, wd_all, row(final_norm), l,
                  final_norm=(l == DEPTH - 1))
    return xf.reshape(batch, seq, d)
```

```python
import functools
import math

import jax
import jax.numpy as jnp
from jax import lax
from jax.experimental import pallas as pl
from jax.experimental.pallas import tpu as pltpu

F32 = jnp.float32
BF16 = jnp.bfloat16

D_MODEL = 2048
DEPTH = 2
CHUNK = 64
CHUNK_SHIFT = CHUNK.bit_length() - 1
HA, A_NOPE, A_ROPE, A_V = 8, 128, 64, 128
Q_RANK, KV_RANK = 384, 128
ROPE_THETA = 10000.0
HB, B_DH = 4, 64
HC, C_DH = 4, 128
LEFT_CHUNKS = 8
REL_CLIP = 256
REL_SIZE = REL_CLIP + CHUNK
D_FF = 4 * D_MODEL
EPS = 1e-6
NEG = -1e30

LANES = 128
LAT_W = 768
PROJ_W = 6 * 512
A_QK = 256
MLA_TQ = 256
FLASH_TK = 512
LOG2E = math.log2(math.e)
VMEM_LIMIT = 56 * 1024 * 1024

NT_DIMS = (((1,), (1,)), ((), ()))


def _params(n_axes, vmem=VMEM_LIMIT):
    return pltpu.CompilerParams(
        dimension_semantics=("arbitrary",) * n_axes, vmem_limit_bytes=vmem)


def _rms(x, g):
    return x * lax.rsqrt(jnp.mean(x * x, axis=-1, keepdims=True) + EPS) * g


def _chunk_ids(shape, dim):
    return lax.broadcasted_iota(jnp.int32, shape, dim) >> CHUNK_SHIFT


def _inproj_kernel(x_ref, g_ref, wl_ref, wp_ref, lat_ref, proj_ref, xn_ref, *, rows):
    for r in range(0, x_ref.shape[0], rows):
        xn_ref[r:r + rows, :] = _rms(x_ref[r:r + rows, :], g_ref[...]).astype(BF16)
    lat_ref[...] = jnp.dot(xn_ref[...], wl_ref[...], preferred_element_type=F32)
    for c in range(0, proj_ref.shape[1], LAT_W):
        proj_ref[:, c:c + LAT_W] = jnp.dot(xn_ref[...], wp_ref[:, c:c + LAT_W],
                                          preferred_element_type=F32).astype(BF16)


def _inproj(x, g, w_lat, w_proj, *, tm=512):
    t, d = x.shape
    const = lambda i: (0, 0)
    return pl.pallas_call(
        functools.partial(_inproj_kernel, rows=256),
        grid=(t // tm,),
        in_specs=[
            pl.BlockSpec((tm, d), lambda i: (i, 0)),
            pl.BlockSpec((1, d), const),
            pl.BlockSpec(w_lat.shape, const),
            pl.BlockSpec(w_proj.shape, const),
        ],
        out_specs=[
            pl.BlockSpec((tm, LAT_W), lambda i: (i, 0)),
            pl.BlockSpec((tm, PROJ_W), lambda i: (i, 0)),
        ],
        out_shape=[
            jax.ShapeDtypeStruct((t, LAT_W), F32),
            jax.ShapeDtypeStruct((t, PROJ_W), BF16),
        ],
        scratch_shapes=[pltpu.VMEM((tm, d), BF16)],
        compiler_params=_params(1),
        name="inproj",
    )(x, g, w_lat, w_proj)


def _mla_prep_kernel(lat_ref, qn_ref, kvn_ref, wuq_ref, wukt_ref, rope_ref,
                     q_ref, k_ref, v_ref):
    tm = lat_ref.shape[0]
    cqn = _rms(lat_ref[:, :Q_RANK], qn_ref[...]).astype(BF16)
    ckvn = _rms(lat_ref[:, Q_RANK:Q_RANK + KV_RANK], kvn_ref[...]).astype(BF16)
    kr = lat_ref[:, Q_RANK + KV_RANK:Q_RANK + KV_RANK + LANES]
    cos = rope_ref[:, :LANES]
    sin_hi = rope_ref[:, LANES:2 * LANES]
    sin_lo = rope_ref[:, 2 * LANES:]

    def rot(r):
        return (r * cos + pltpu.roll(r, A_ROPE // 2, 1) * sin_hi
                + pltpu.roll(r, LANES - A_ROPE // 2, 1) * sin_lo)

    scale = (A_NOPE + A_ROPE) ** -0.5 * LOG2E
    q = jnp.dot(cqn, wuq_ref[...], preferred_element_type=F32)
    for h in range(HA):
        c0 = A_QK * h
        qa = jnp.dot(q[:, c0:c0 + LANES].astype(BF16), wukt_ref[h],
                     preferred_element_type=F32)
        qa = (qa * scale).astype(BF16)
        qr = (rot(q[:, c0 + LANES:c0 + A_QK]) * scale).astype(BF16)
        for blk in range(tm // MLA_TQ):
            r0 = (blk * HA + h) * MLA_TQ
            q_ref[r0:r0 + MLA_TQ, :LANES] = qa[blk * MLA_TQ:(blk + 1) * MLA_TQ]
            q_ref[r0:r0 + MLA_TQ, LANES:] = qr[blk * MLA_TQ:(blk + 1) * MLA_TQ]
    k_ref[:, :LANES] = ckvn
    k_ref[:, LANES:] = rot(kr).astype(BF16)
    v_ref[:, :LANES] = ckvn
    v_ref[:, LANES:] = jnp.ones((tm, LANES), BF16)


def _mla_prep(lat, qn, kvn, wuq, wukt, rope, *, seq, tm=512):
    t = lat.shape[0]
    const = lambda i: (0, 0)
    return pl.pallas_call(
        _mla_prep_kernel,
        grid=(t // tm,),
        in_specs=[
            pl.BlockSpec((tm, LAT_W), lambda i: (i, 0)),
            pl.BlockSpec((1, Q_RANK), const),
            pl.BlockSpec((1, KV_RANK), const),
            pl.BlockSpec(wuq.shape, const),
            pl.BlockSpec(wukt.shape, lambda i: (0, 0, 0)),
            pl.BlockSpec((tm, 3 * LANES), lambda i: (i % (seq // tm), 0)),
        ],
        out_specs=[
            pl.BlockSpec((tm * HA, A_QK), lambda i: (i, 0)),
            pl.BlockSpec((tm, A_QK), lambda i: (i, 0)),
            pl.BlockSpec((tm, A_QK), lambda i: (i, 0)),
        ],
        out_shape=[
            jax.ShapeDtypeStruct((t * HA, A_QK), BF16),
            jax.ShapeDtypeStruct((t, A_QK), BF16),
            jax.ShapeDtypeStruct((t, A_QK), BF16),
        ],
        compiler_params=_params(1),
        name="mla_prep",
    )(lat, qn, kvn, wuq, wukt, rope)


def _flash_ops(q_ref, k_ref, v_ref, sa_ref, sb_ref, m_ref, acc_ref, exp_fn):
    tk = sa_ref.shape[1]
    everything = slice(None)

    def qk(t, dst_ref, rows=everything):
        off = pl.multiple_of(t * tk, tk)
        dst_ref[rows, :] = lax.dot_general(q_ref[rows, :], k_ref[pl.ds(off, tk), :], NT_DIMS,
                                           preferred_element_type=F32)

    def softmax_pv(t, src_ref, bias=None, rows=everything):
        s = src_ref[rows, :]
        if bias is not None:
            nb = bias.shape[0]
            s = jnp.concatenate([s[:nb] + bias, s[nb:]], axis=0) if nb < s.shape[0] else s + bias
        m_prev = m_ref[rows, :]
        m_next = jnp.maximum(m_prev, jnp.max(s, axis=1)[:, None])
        p = exp_fn(s - jnp.tile(m_next, (1, tk // LANES)))
        alpha = exp_fn(m_prev - m_next)
        off = pl.multiple_of(t * tk, tk)
        pv = jnp.dot(p.astype(BF16), v_ref[pl.ds(off, tk), :], preferred_element_type=F32)
        acc_ref[rows, :] = jnp.tile(alpha, (1, 2)) * acc_ref[rows, :] + pv
        m_ref[rows, :] = m_next

    def pair(t):
        qk(t + 1, sb_ref)
        softmax_pv(t, sa_ref)
        qk(t + 2, sa_ref)
        softmax_pv(t + 1, sb_ref)

    def start_and_full_pairs(npairs):
        m_ref[...] = jnp.full(m_ref.shape, NEG, F32)
        acc_ref[...] = jnp.zeros(acc_ref.shape, F32)
        qk(0, sa_ref)

        def quad(j, carry):
            pair(4 * j)
            pair(4 * j + 2)
            return carry

        lax.fori_loop(0, npairs >> 1, quad, 0)

        @pl.when((npairs & 1) != 0)
        def _():
            pair(2 * (npairs - 1))

    return qk, softmax_pv, start_and_full_pairs


def _flash_scratch(rows):
    return [pltpu.VMEM((rows, FLASH_TK), F32), pltpu.VMEM((rows, FLASH_TK), F32),
            pltpu.VMEM((rows, LANES), F32), pltpu.VMEM((rows, 2 * LANES), F32)]


def _mla_attn_kernel(q_ref, k_ref, v_ref, wuv_ref, o_ref, sa_ref, sb_ref, m_ref, acc_ref):
    i = pl.program_id(1)
    n = (i * MLA_TQ) // FLASH_TK

    def last_bias():
        shape = (MLA_TQ, FLASH_TK)
        rows = i * MLA_TQ + lax.broadcasted_iota(jnp.int32, shape, 0)
        cols = n * FLASH_TK + lax.broadcasted_iota(jnp.int32, shape, 1)
        bias = jnp.where((cols >> CHUNK_SHIFT) <= (rows >> CHUNK_SHIFT), 0.0, NEG)
        return jnp.tile(bias, (HA, 1))

    qk, softmax_pv, start_and_full_pairs = _flash_ops(
        q_ref, k_ref, v_ref, sa_ref, sb_ref, m_ref, acc_ref, jnp.exp2)
    start_and_full_pairs(n >> 1)

    @pl.when((n & 1) == 0)
    def _():
        softmax_pv(n, sa_ref, last_bias())

    @pl.when((n & 1) == 1)
    def _():
        qk(n, sb_ref)
        softmax_pv(n - 1, sa_ref)
        softmax_pv(n, sb_ref, last_bias())

    for h in range(HA):
        a = acc_ref[h * MLA_TQ:(h + 1) * MLA_TQ, :]
        o = (a[:, :LANES] / a[:, LANES:]).astype(BF16)
        o_ref[:, h * A_V:(h + 1) * A_V] = jnp.dot(
            o, wuv_ref[h], preferred_element_type=F32).astype(BF16)


def _mla_attn(q, k, v, wuv, *, batch, seq):
    nq = seq // MLA_TQ
    rows = HA * MLA_TQ
    return pl.pallas_call(
        _mla_attn_kernel,
        grid=(batch, nq),
        in_specs=[
            pl.BlockSpec((rows, A_QK), lambda b, i: (b * nq + i, 0)),
            pl.BlockSpec((seq, A_QK), lambda b, i: (b, 0)),
            pl.BlockSpec((seq, A_QK), lambda b, i: (b, 0)),
            pl.BlockSpec(wuv.shape, lambda b, i: (0, 0, 0)),
        ],
        out_specs=pl.BlockSpec((MLA_TQ, HA * A_V), lambda b, i: (b * nq + i, 0)),
        out_shape=jax.ShapeDtypeStruct((batch * seq, HA * A_V), BF16),
        scratch_shapes=_flash_scratch(rows),
        compiler_params=_params(2),
        name="mla_attn",
    )(q, k, v, wuv)


def _diff_attn_kernel(q_ref, k_ref, v_ref, lq1_ref, lk1_ref, lq2_ref, lk2_ref, g_ref, o_ref,
                      kaug_ref, vaug_ref, dbias_ref, qs_ref, sa_ref, sb_ref, m_ref, acc_ref,
                      *, lam_init):
    h = pl.program_id(1)
    i = pl.program_id(2)
    tq = FLASH_TK
    lane = lax.broadcasted_iota(jnp.int32, (tq, LANES), 1)
    slope_bits = (127 - (8 // HB) * (h + 1)) << 23
    slope = lax.bitcast_convert_type(jnp.full((1, LANES), slope_bits, jnp.int32), F32)

    @pl.when(i == 0)
    def _():
        r = lax.broadcasted_iota(jnp.int32, (tq, tq), 0)
        c = lax.broadcasted_iota(jnp.int32, (tq, tq), 1)
        ahead = jnp.maximum(c - r, 0).astype(F32)
        bias = jnp.tile(slope, (1, tq // LANES)) * (-2.0 * ahead)
        dbias_ref[...] = jnp.where((c >> CHUNK_SHIFT) <= (r >> CHUNK_SHIFT), bias, NEG)

        def fill(c, carry):
            off = pl.multiple_of(c * tq, tq)
            pos = off + lax.broadcasted_iota(jnp.int32, (tq, LANES), 0)
            cols = jnp.where(lane == 0, pos >> CHUNK_SHIFT,
                             jnp.where(lane == 1, pos & (CHUNK - 1), 0))
            kaug_ref[pl.ds(off, tq), :LANES] = k_ref[pl.ds(off, tq), :]
            kaug_ref[pl.ds(off, tq), LANES:] = cols.astype(F32).astype(BF16)
            vaug_ref[pl.ds(off, tq), :LANES] = v_ref[pl.ds(off, tq), :]
            vaug_ref[pl.ds(off, tq), LANES:] = jnp.ones((tq, LANES), BF16)
            return carry

        lax.fori_loop(0, k_ref.shape[0] // tq, fill, 0)

    pos_cols = jnp.where(lane == 0, slope * CHUNK, jnp.where(lane == 1, slope, 0.0)).astype(BF16)
    for half in range(2):
        q = q_ref[half * tq:(half + 1) * tq, :].astype(F32) * (B_DH ** -0.5)
        r0 = 2 * half * tq
        qs_ref[r0:r0 + tq, :LANES] = jnp.where(lane < B_DH, q, 0.0).astype(BF16)
        qs_ref[r0 + tq:r0 + 2 * tq, :LANES] = jnp.where(lane >= B_DH, q, 0.0).astype(BF16)
        qs_ref[r0:r0 + tq, LANES:] = pos_cols
        qs_ref[r0 + tq:r0 + 2 * tq, LANES:] = pos_cols

    qk, softmax_pv, start_and_full_pairs = _flash_ops(
        qs_ref, kaug_ref, vaug_ref, sa_ref, sb_ref, m_ref, acc_ref, jnp.exp)
    start_and_full_pairs(i)
    bottom = slice(2 * tq, 4 * tq)
    diag_bias = jnp.tile(dbias_ref[...], (2, 1))
    qk(2 * i + 1, sb_ref, rows=bottom)
    softmax_pv(2 * i, sa_ref, diag_bias)
    softmax_pv(2 * i + 1, sb_ref, diag_bias, rows=bottom)

    lam = (jnp.exp(jnp.sum(lq1_ref[...] * lk1_ref[...], axis=-1, keepdims=True))
           - jnp.exp(jnp.sum(lq2_ref[...] * lk2_ref[...], axis=-1, keepdims=True))
           + lam_init)
    for half in range(2):
        r0 = 2 * half * tq
        o1 = acc_ref[r0:r0 + tq, :LANES] / acc_ref[r0:r0 + tq, LANES:]
        o2 = acc_ref[r0 + tq:r0 + 2 * tq, :LANES] / acc_ref[r0 + tq:r0 + 2 * tq, LANES:]
        o = o1 - lam * o2
        o_ref[half * tq:(half + 1) * tq, :] = (
            _rms(o, g_ref[...]) * (1.0 - lam_init)).astype(BF16)


def _diff_attn(proj, lq1, lk1, lq2, lk2, subln, *, batch, seq, lam_init):
    tq = 2 * FLASH_TK
    nq = seq // tq
    vec = pl.BlockSpec((1, B_DH), lambda b, h, i: (0, 0))
    return pl.pallas_call(
        functools.partial(_diff_attn_kernel, lam_init=lam_init),
        grid=(batch, HB, nq),
        in_specs=[
            pl.BlockSpec((tq, LANES), lambda b, h, i: (b * nq + i, h)),
            pl.BlockSpec((seq, LANES), lambda b, h, i: (b, HB + h)),
            pl.BlockSpec((seq, LANES), lambda b, h, i: (b, 2 * HB + h)),
            vec, vec, vec, vec,
            pl.BlockSpec((1, 2 * B_DH), lambda b, h, i: (0, 0)),
        ],
        out_specs=pl.BlockSpec((tq, LANES), lambda b, h, i: (b * nq + i, h)),
        out_shape=jax.ShapeDtypeStruct((batch * seq, HB * 2 * B_DH), BF16),
        scratch_shapes=[pltpu.VMEM((seq, 2 * LANES), BF16), pltpu.VMEM((seq, 2 * LANES), BF16),
                        pltpu.VMEM((FLASH_TK, FLASH_TK), F32),
                        pltpu.VMEM((2 * tq, 2 * LANES), BF16)] + _flash_scratch(2 * tq),
        compiler_params=_params(3),
        name="diff_attn",
    )(proj, proj, proj, lq1, lk1, lq2, lk2, subln)


def _band_attn_kernel(q_ref, k_ref, v_ref, base_ref, o_ref, bias_ref, s_ref, *, gq, sub):
    b = pl.program_id(1)
    g2 = pl.program_id(2)
    win = 2 * gq

    @pl.when(jnp.logical_and(b == 0, g2 == 0))
    def _():
        x = jnp.broadcast_to(base_ref[0], (gq, win))
        x = pltpu.roll(x, 0, 1, stride=1, stride_axis=0)
        d = _chunk_ids((gq, win), 1) - _chunk_ids((gq, win), 0)
        bias_ref[...] = jnp.where(jnp.logical_and(d >= 0, d <= LEFT_CHUNKS), x * LOG2E, NEG)

    def attend(half, keys, bias_cols):
        width = bias_cols.stop - bias_cols.start
        q = q_ref[half * gq:(half + 1) * gq, :].astype(F32) * (C_DH ** -0.5 * LOG2E)
        s_ref[half, :, :width] = lax.dot_general(q.astype(BF16), k_ref[keys, :], NT_DIMS,
                                                 preferred_element_type=F32)
        v = v_ref[keys, :]
        for r0 in range(0, gq, sub):
            s = s_ref[half, r0:r0 + sub, :width] + bias_ref[r0:r0 + sub, bias_cols]
            p = jnp.exp2(s - jnp.max(s, axis=-1, keepdims=True))
            l = jnp.sum(p, axis=-1, keepdims=True)
            o = jnp.dot(p.astype(BF16), v, preferred_element_type=F32)
            o_ref[half * gq + r0:half * gq + r0 + sub, :] = (o / l).astype(BF16)

    @pl.when(g2 == 0)
    def _():
        attend(0, slice(0, gq), slice(gq, win))
        attend(1, slice(0, win), slice(0, win))

    @pl.when(g2 > 0)
    def _():
        off0 = pl.multiple_of((2 * g2 - 1) * gq, gq)
        off1 = pl.multiple_of(2 * g2 * gq, gq)
        attend(0, pl.ds(off0, win), slice(0, win))
        attend(1, pl.ds(off1, win), slice(0, win))


def _band_attn(proj, base, *, batch, seq):
    gq = LEFT_CHUNKS * CHUNK
    ng = seq // (2 * gq)
    qc, kc, vc = 3 * HB, 3 * HB + HC, 3 * HB + 2 * HC
    return pl.pallas_call(
        functools.partial(_band_attn_kernel, gq=gq, sub=128),
        grid=(HC, batch, ng),
        in_specs=[
            pl.BlockSpec((2 * gq, C_DH), lambda h, b, g: (b * ng + g, qc + h)),
            pl.BlockSpec((seq, C_DH), lambda h, b, g: (b, kc + h)),
            pl.BlockSpec((seq, C_DH), lambda h, b, g: (b, vc + h)),
            pl.BlockSpec((1, 1, 2 * gq), lambda h, b, g: (h, 0, 0)),
        ],
        out_specs=pl.BlockSpec((2 * gq, C_DH), lambda h, b, g: (b * ng + g, h)),
        out_shape=jax.ShapeDtypeStruct((batch * seq, HC * C_DH), BF16),
        scratch_shapes=[pltpu.VMEM((gq, 2 * gq), F32), pltpu.VMEM((2, gq, 2 * gq), F32)],
        compiler_params=_params(3),
        name="band_attn",
    )(proj, proj, proj, base)


def _oproj_kernel(a_ref, b_ref, c_ref, x_ref, w_ref, o_ref):
    na, nb = a_ref.shape[1], a_ref.shape[1] + b_ref.shape[1]
    acc = jnp.dot(a_ref[...], w_ref[:na, :], preferred_element_type=F32)
    acc += jnp.dot(b_ref[...], w_ref[na:nb, :], preferred_element_type=F32)
    acc += jnp.dot(c_ref[...], w_ref[nb:, :], preferred_element_type=F32)
    o_ref[...] = x_ref[...] + acc


def _oproj(a, b, c, x, w, layer, *, tm=512):
    t, d = x.shape
    row = lambda i: (i, 0)
    return pl.pallas_call(
        _oproj_kernel,
        grid=(t // tm,),
        in_specs=[
            pl.BlockSpec((tm, a.shape[1]), row),
            pl.BlockSpec((tm, b.shape[1]), row),
            pl.BlockSpec((tm, c.shape[1]), row),
            pl.BlockSpec((tm, d), row),
            pl.BlockSpec((None,) + w.shape[1:], lambda i: (layer, 0, 0)),
        ],
        out_specs=pl.BlockSpec((tm, d), row),
        out_shape=jax.ShapeDtypeStruct((t, d), F32),
        compiler_params=_params(1),
        name="oproj",
    )(a, b, c, x, w)


def _mlp_kernel(x_ref, g_ref, wu_ref, wd_ref, gf_ref, o_ref, xn_ref, *, rows, final_norm):
    f = pl.program_id(1)

    @pl.when(f == 0)
    def _():
        for r in range(0, x_ref.shape[0], rows):
            xr = x_ref[r:r + rows, :]
            xn_ref[r:r + rows, :] = _rms(xr, g_ref[...]).astype(BF16)
            o_ref[r:r + rows, :] = xr

    hidden = jnp.dot(xn_ref[...], wu_ref[...], preferred_element_type=F32)
    hidden = jnp.square(jnp.maximum(hidden, 0.0)).astype(BF16)
    o_ref[...] += jnp.dot(hidden, wd_ref[...], preferred_element_type=F32)

    if final_norm:
        @pl.when(f == pl.num_programs(1) - 1)
        def _():
            for r in range(0, x_ref.shape[0], rows):
                o_ref[r:r + rows, :] = _rms(o_ref[r:r + rows, :], gf_ref[...])


def _mlp(x, g, wu, wd, gf, layer, *, final_norm, tm=1024, tf=512):
    t, d = x.shape
    dff = wu.shape[2]
    return pl.pallas_call(
        functools.partial(_mlp_kernel, rows=256, final_norm=final_norm),
        grid=(t // tm, dff // tf),
        in_specs=[
            pl.BlockSpec((tm, d), lambda i, f: (i, 0)),
            pl.BlockSpec((1, d), lambda i, f: (0, 0)),
            pl.BlockSpec((None, d, tf), lambda i, f: (layer, 0, f)),
            pl.BlockSpec((None, tf, d), lambda i, f: (layer, f, 0)),
            pl.BlockSpec((1, d), lambda i, f: (0, 0)),
        ],
        out_specs=pl.BlockSpec((tm, d), lambda i, f: (i, 0)),
        out_shape=jax.ShapeDtypeStruct((t, d), F32),
        scratch_shapes=[pltpu.VMEM((tm, d), BF16)],
        compiler_params=_params(2),
        name="mlp",
    )(x, g, wu, wd, gf)


def _rope_tables(seq):
    half = A_ROPE // 2
    lane = jnp.arange(LANES)
    inv_freq = ROPE_THETA ** (-(lane % half).astype(F32) / half)
    ang = jnp.arange(seq, dtype=F32)[:, None] * inv_freq[None, :]
    cos, sin = jnp.cos(ang), jnp.sin(ang)
    return jnp.concatenate([
        jnp.where(lane < A_ROPE, cos, 0.0),
        jnp.where((lane >= half) & (lane < A_ROPE), sin, 0.0),
        jnp.where(lane < half, -sin, 0.0)], axis=1)


def _split_in_weight(w_in):
    lat = Q_RANK + KV_RANK + A_ROPE
    w_lat = jnp.pad(w_in[:, :lat], ((0, 0), (0, LAT_W - lat))).astype(BF16)
    return w_lat, w_in[:, lat:].astype(BF16)


def _band_base(rel_bias):
    gq = LEFT_CHUNKS * CHUNK
    edge = rel_bias[:, REL_SIZE - 1:]
    head = jnp.broadcast_to(edge, (HC, gq - REL_CLIP))
    tail = jnp.broadcast_to(edge, (HC, 2 * gq - (gq - REL_CLIP) - REL_SIZE))
    return jnp.concatenate([head, rel_bias[:, ::-1], tail], axis=1).reshape(HC, 1, 2 * gq)


def kernel(x, attn_norm, w_in, q_a_norm, kv_a_norm, w_uq, w_ukv, lambda_q1, lambda_k1,
           lambda_q2, lambda_k2, diff_subln, rel_bias, w_o, mlp_norm, w_up, w_down, final_norm):
    batch, seq, d = x.shape
    xf = x.reshape(batch * seq, d)
    rope = _rope_tables(seq)
    row = lambda v: v.reshape(1, -1)
    wo_all, wu_all, wd_all = w_o.astype(BF16), w_up.astype(BF16), w_down.astype(BF16)
    for l in range(DEPTH):
        lat, proj = _inproj(xf, row(attn_norm[l]), *_split_in_weight(w_in[l]))

        wuq = w_uq[l].reshape(Q_RANK, HA, A_NOPE + A_ROPE)
        wuq = jnp.pad(wuq, ((0, 0), (0, 0), (0, A_QK - A_NOPE - A_ROPE)))
        wuq = wuq.reshape(Q_RANK, HA * A_QK).astype(BF16)
        wukv = w_ukv[l].reshape(KV_RANK, HA, A_NOPE + A_V)
        wukt = wukv[:, :, :A_NOPE].transpose(1, 2, 0).astype(BF16)
        wuv = wukv[:, :, A_NOPE:].transpose(1, 0, 2).astype(BF16)
        q, k, v = _mla_prep(lat, row(q_a_norm[l]), row(kv_a_norm[l]), wuq, wukt, rope, seq=seq)
        out_a = _mla_attn(q, k, v, wuv, batch=batch, seq=seq)

        lam_init = 0.8 - 0.6 * math.exp(-0.3 * l)
        out_b = _diff_attn(proj, row(lambda_q1[l]), row(lambda_k1[l]), row(lambda_q2[l]),
                           row(lambda_k2[l]), row(diff_subln[l]),
                           batch=batch, seq=seq, lam_init=lam_init)
        out_c = _band_attn(proj, _band_base(rel_bias[l]), batch=batch, seq=seq)

        xf = _oproj(out_a, out_b, out_c, xf, wo_all, l)
        xf = _mlp(xf, row(mlp_norm[l]), wu_all, wd_all, row(final_norm), l,
                  final_norm=(l == DEPTH - 1))
    return xf.reshape(batch, seq, d)
```

```python
import functools
import math

import jax
import jax.numpy as jnp
from jax import lax
from jax.experimental import pallas as pl
from jax.experimental.pallas import tpu as pltpu

F32 = jnp.float32
BF16 = jnp.bfloat16

D_MODEL = 2048
DEPTH = 2
CHUNK = 64
CHUNK_SHIFT = CHUNK.bit_length() - 1
HA, A_NOPE, A_ROPE, A_V = 8, 128, 64, 128
Q_RANK, KV_RANK = 384, 128
ROPE_THETA = 10000.0
HB, B_DH = 4, 64
HC, C_DH = 4, 128
LEFT_CHUNKS = 8
REL_CLIP = 256
REL_SIZE = REL_CLIP + CHUNK
D_FF = 4 * D_MODEL
EPS = 1e-6
NEG = -1e30

LANES = 128
LAT_W = 768
PROJ_W = 6 * 512
A_QK = 256
MLA_TQ = 256
FLASH_TK = 512
MLP_TF = 512
LOG2E = math.log2(math.e)
VMEM_LIMIT = 56 * 1024 * 1024

NT_DIMS = (((1,), (1,)), ((), ()))


def _params(n_axes, vmem=VMEM_LIMIT):
    return pltpu.CompilerParams(
        dimension_semantics=("arbitrary",) * n_axes, vmem_limit_bytes=vmem)


def _rms(x, g):
    return x * lax.rsqrt(jnp.mean(x * x, axis=-1, keepdims=True) + EPS) * g


def _chunk_ids(shape, dim):
    return lax.broadcasted_iota(jnp.int32, shape, dim) >> CHUNK_SHIFT


def _inproj_kernel(x_ref, g_ref, wl_ref, wp_ref, lat_ref, proj_ref, xn_ref, *, rows):
    for r in range(0, x_ref.shape[0], rows):
        xn_ref[r:r + rows, :] = _rms(x_ref[r:r + rows, :], g_ref[...]).astype(BF16)
    lat_ref[...] = jnp.dot(xn_ref[...], wl_ref[...], preferred_element_type=F32)
    for c in range(0, proj_ref.shape[1], LAT_W):
        proj_ref[:, c:c + LAT_W] = jnp.dot(xn_ref[...], wp_ref[:, c:c + LAT_W],
                                          preferred_element_type=F32).astype(BF16)


def _inproj(x, g, w_lat, w_proj, *, tm=512):
    t, d = x.shape
    const = lambda i: (0, 0)
    return pl.pallas_call(
        functools.partial(_inproj_kernel, rows=256),
        grid=(t // tm,),
        in_specs=[
            pl.BlockSpec((tm, d), lambda i: (i, 0)),
            pl.BlockSpec((1, d), const),
            pl.BlockSpec(w_lat.shape, const),
            pl.BlockSpec(w_proj.shape, const),
        ],
        out_specs=[
            pl.BlockSpec((tm, LAT_W), lambda i: (i, 0)),
            pl.BlockSpec((tm, PROJ_W), lambda i: (i, 0)),
        ],
        out_shape=[
            jax.ShapeDtypeStruct((t, LAT_W), F32),
            jax.ShapeDtypeStruct((t, PROJ_W), BF16),
        ],
        scratch_shapes=[pltpu.VMEM((tm, d), BF16)],
        compiler_params=_params(1),
        name="inproj",
    )(x, g, w_lat, w_proj)


def _mla_prep_kernel(lat_ref, qn_ref, kvn_ref, wuq_ref, wukt_ref, rope_ref,
                     q_ref, k_ref, v_ref):
    tm = lat_ref.shape[0]
    cqn = _rms(lat_ref[:, :Q_RANK], qn_ref[...]).astype(BF16)
    ckvn = _rms(lat_ref[:, Q_RANK:Q_RANK + KV_RANK], kvn_ref[...]).astype(BF16)
    kr = lat_ref[:, Q_RANK + KV_RANK:Q_RANK + KV_RANK + LANES]
    cos = rope_ref[:, :LANES]
    sin_hi = rope_ref[:, LANES:2 * LANES]
    sin_lo = rope_ref[:, 2 * LANES:]

    def rot(r):
        return (r * cos + pltpu.roll(r, A_ROPE // 2, 1) * sin_hi
                + pltpu.roll(r, LANES - A_ROPE // 2, 1) * sin_lo)

    scale = (A_NOPE + A_ROPE) ** -0.5 * LOG2E
    q = jnp.dot(cqn, wuq_ref[...], preferred_element_type=F32)
    for h in range(HA):
        c0 = A_QK * h
        qa = jnp.dot(q[:, c0:c0 + LANES].astype(BF16), wukt_ref[h],
                     preferred_element_type=F32)
        qa = (qa * scale).astype(BF16)
        qr = (rot(q[:, c0 + LANES:c0 + A_QK]) * scale).astype(BF16)
        for blk in range(tm // MLA_TQ):
            r0 = (blk * HA + h) * MLA_TQ
            q_ref[r0:r0 + MLA_TQ, :LANES] = qa[blk * MLA_TQ:(blk + 1) * MLA_TQ]
            q_ref[r0:r0 + MLA_TQ, LANES:] = qr[blk * MLA_TQ:(blk + 1) * MLA_TQ]
    k_ref[:, :LANES] = ckvn
    k_ref[:, LANES:] = rot(kr).astype(BF16)
    v_ref[:, :LANES] = ckvn
    v_ref[:, LANES:] = jnp.ones((tm, LANES), BF16)


def _mla_prep(lat, qn, kvn, wuq, wukt, rope, *, seq, tm=512):
    t = lat.shape[0]
    const = lambda i: (0, 0)
    return pl.pallas_call(
        _mla_prep_kernel,
        grid=(t // tm,),
        in_specs=[
            pl.BlockSpec((tm, LAT_W), lambda i: (i, 0)),
            pl.BlockSpec((1, Q_RANK), const),
            pl.BlockSpec((1, KV_RANK), const),
            pl.BlockSpec(wuq.shape, const),
            pl.BlockSpec(wukt.shape, lambda i: (0, 0, 0)),
            pl.BlockSpec((tm, 3 * LANES), lambda i: (i % (seq // tm), 0)),
        ],
        out_specs=[
            pl.BlockSpec((tm * HA, A_QK), lambda i: (i, 0)),
            pl.BlockSpec((tm, A_QK), lambda i: (i, 0)),
            pl.BlockSpec((tm, A_QK), lambda i: (i, 0)),
        ],
        out_shape=[
            jax.ShapeDtypeStruct((t * HA, A_QK), BF16),
            jax.ShapeDtypeStruct((t, A_QK), BF16),
            jax.ShapeDtypeStruct((t, A_QK), BF16),
        ],
        compiler_params=_params(1),
        name="mla_prep",
    )(lat, qn, kvn, wuq, wukt, rope)


def _flash_ops(q_ref, k_ref, v_ref, sa_ref, sb_ref, m_ref, acc_ref, exp_fn):
    tk = sa_ref.shape[1]
    everything = slice(None)

    def qk(t, dst_ref, rows=everything):
        off = pl.multiple_of(t * tk, tk)
        dst_ref[rows, :] = lax.dot_general(q_ref[rows, :], k_ref[pl.ds(off, tk), :], NT_DIMS,
                                           preferred_element_type=F32)

    def softmax_pv(t, src_ref, bias=None, rows=everything):
        s = src_ref[rows, :]
        if bias is not None:
            nb = bias.shape[0]
            s = jnp.concatenate([s[:nb] + bias, s[nb:]], axis=0) if nb < s.shape[0] else s + bias
        m_prev = m_ref[rows, :]
        m_next = jnp.maximum(m_prev, jnp.max(s, axis=1)[:, None])
        p = exp_fn(s - jnp.tile(m_next, (1, tk // LANES)))
        alpha = exp_fn(m_prev - m_next)
        off = pl.multiple_of(t * tk, tk)
        pv = jnp.dot(p.astype(BF16), v_ref[pl.ds(off, tk), :], preferred_element_type=F32)
        acc_ref[rows, :] = jnp.tile(alpha, (1, 2)) * acc_ref[rows, :] + pv
        m_ref[rows, :] = m_next

    def pair(t):
        qk(t + 1, sb_ref)
        softmax_pv(t, sa_ref)
        qk(t + 2, sa_ref)
        softmax_pv(t + 1, sb_ref)

    def start_and_full_pairs(npairs):
        m_ref[...] = jnp.full(m_ref.shape, NEG, F32)
        acc_ref[...] = jnp.zeros(acc_ref.shape, F32)
        qk(0, sa_ref)

        def quad(j, carry):
            pair(4 * j)
            pair(4 * j + 2)
            return carry

        lax.fori_loop(0, npairs >> 1, quad, 0)

        @pl.when((npairs & 1) != 0)
        def _():
            pair(2 * (npairs - 1))

    return qk, softmax_pv, start_and_full_pairs


def _flash_scratch(rows):
    return [pltpu.VMEM((rows, FLASH_TK), F32), pltpu.VMEM((rows, FLASH_TK), F32),
            pltpu.VMEM((rows, LANES), F32), pltpu.VMEM((rows, 2 * LANES), F32)]


def _mla_attn_kernel(q_ref, k_ref, v_ref, wuv_ref, o_ref, sa_ref, sb_ref, m_ref, acc_ref):
    i = pl.program_id(1)
    n = (i * MLA_TQ) // FLASH_TK

    def last_bias():
        shape = (MLA_TQ, FLASH_TK)
        rows = i * MLA_TQ + lax.broadcasted_iota(jnp.int32, shape, 0)
        cols = n * FLASH_TK + lax.broadcasted_iota(jnp.int32, shape, 1)
        bias = jnp.where((cols >> CHUNK_SHIFT) <= (rows >> CHUNK_SHIFT), 0.0, NEG)
        return jnp.tile(bias, (HA, 1))

    qk, softmax_pv, start_and_full_pairs = _flash_ops(
        q_ref, k_ref, v_ref, sa_ref, sb_ref, m_ref, acc_ref, jnp.exp2)
    start_and_full_pairs(n >> 1)

    @pl.when((n & 1) == 0)
    def _():
        softmax_pv(n, sa_ref, last_bias())

    @pl.when((n & 1) == 1)
    def _():
        qk(n, sb_ref)
        softmax_pv(n - 1, sa_ref)
        softmax_pv(n, sb_ref, last_bias())

    for h in range(HA):
        a = acc_ref[h * MLA_TQ:(h + 1) * MLA_TQ, :]
        o = (a[:, :LANES] / a[:, LANES:]).astype(BF16)
        o_ref[:, h * A_V:(h + 1) * A_V] = jnp.dot(
            o, wuv_ref[h], preferred_element_type=F32).astype(BF16)


def _mla_attn(q, k, v, wuv, *, batch, seq):
    nq = seq // MLA_TQ
    rows = HA * MLA_TQ
    return pl.pallas_call(
        _mla_attn_kernel,
        grid=(batch, nq),
        in_specs=[
            pl.BlockSpec((rows, A_QK), lambda b, i: (b * nq + i, 0)),
            pl.BlockSpec((seq, A_QK), lambda b, i: (b, 0)),
            pl.BlockSpec((seq, A_QK), lambda b, i: (b, 0)),
            pl.BlockSpec(wuv.shape, lambda b, i: (0, 0, 0)),
        ],
        out_specs=pl.BlockSpec((MLA_TQ, HA * A_V), lambda b, i: (b * nq + i, 0)),
        out_shape=jax.ShapeDtypeStruct((batch * seq, HA * A_V), BF16),
        scratch_shapes=_flash_scratch(rows),
        compiler_params=_params(2),
        name="mla_attn",
    )(q, k, v, wuv)


def _diff_attn_kernel(q_ref, k_ref, v_ref, lq1_ref, lk1_ref, lq2_ref, lk2_ref, g_ref, o_ref,
                      kaug_ref, vaug_ref, dbias_ref, qs_ref, sa_ref, sb_ref, m_ref, acc_ref,
                      *, lam_init):
    h = pl.program_id(1)
    i = pl.program_id(2)
    tq = FLASH_TK
    lane = lax.broadcasted_iota(jnp.int32, (tq, LANES), 1)
    slope_bits = (127 - (8 // HB) * (h + 1)) << 23
    slope = lax.bitcast_convert_type(jnp.full((1, LANES), slope_bits, jnp.int32), F32)

    @pl.when(i == 0)
    def _():
        r = lax.broadcasted_iota(jnp.int32, (tq, tq), 0)
        c = lax.broadcasted_iota(jnp.int32, (tq, tq), 1)
        ahead = jnp.maximum(c - r, 0).astype(F32)
        bias = jnp.tile(slope, (1, tq // LANES)) * (-2.0 * ahead)
        dbias_ref[...] = jnp.where((c >> CHUNK_SHIFT) <= (r >> CHUNK_SHIFT), bias, NEG)

        def fill(c, carry):
            off = pl.multiple_of(c * tq, tq)
            pos = off + lax.broadcasted_iota(jnp.int32, (tq, LANES), 0)
            cols = jnp.where(lane == 0, pos >> CHUNK_SHIFT,
                             jnp.where(lane == 1, pos & (CHUNK - 1), 0))
            kaug_ref[pl.ds(off, tq), :LANES] = k_ref[pl.ds(off, tq), :]
            kaug_ref[pl.ds(off, tq), LANES:] = cols.astype(F32).astype(BF16)
            vaug_ref[pl.ds(off, tq), :LANES] = v_ref[pl.ds(off, tq), :]
            vaug_ref[pl.ds(off, tq), LANES:] = jnp.ones((tq, LANES), BF16)
            return carry

        lax.fori_loop(0, k_ref.shape[0] // tq, fill, 0)

    pos_cols = jnp.where(lane == 0, slope * CHUNK, jnp.where(lane == 1, slope, 0.0)).astype(BF16)
    for half in range(2):
        q = q_ref[half * tq:(half + 1) * tq, :].astype(F32) * (B_DH ** -0.5)
        r0 = 2 * half * tq
        qs_ref[r0:r0 + tq, :LANES] = jnp.where(lane < B_DH, q, 0.0).astype(BF16)
        qs_ref[r0 + tq:r0 + 2 * tq, :LANES] = jnp.where(lane >= B_DH, q, 0.0).astype(BF16)
        qs_ref[r0:r0 + tq, LANES:] = pos_cols
        qs_ref[r0 + tq:r0 + 2 * tq, LANES:] = pos_cols

    qk, softmax_pv, start_and_full_pairs = _flash_ops(
        qs_ref, kaug_ref, vaug_ref, sa_ref, sb_ref, m_ref, acc_ref, jnp.exp)
    start_and_full_pairs(i)
    bottom = slice(2 * tq, 4 * tq)
    diag_bias = jnp.tile(dbias_ref[...], (2, 1))
    qk(2 * i + 1, sb_ref, rows=bottom)
    softmax_pv(2 * i, sa_ref, diag_bias)
    softmax_pv(2 * i + 1, sb_ref, diag_bias, rows=bottom)

    lam = (jnp.exp(jnp.sum(lq1_ref[...] * lk1_ref[...], axis=-1, keepdims=True))
           - jnp.exp(jnp.sum(lq2_ref[...] * lk2_ref[...], axis=-1, keepdims=True))
           + lam_init)
    for half in range(2):
        r0 = 2 * half * tq
        o1 = acc_ref[r0:r0 + tq, :LANES] / acc_ref[r0:r0 + tq, LANES:]
        o2 = acc_ref[r0 + tq:r0 + 2 * tq, :LANES] / acc_ref[r0 + tq:r0 + 2 * tq, LANES:]
        o = o1 - lam * o2
        o_ref[half * tq:(half + 1) * tq, :] = (
            _rms(o, g_ref[...]) * (1.0 - lam_init)).astype(BF16)


def _diff_attn(proj, lq1, lk1, lq2, lk2, subln, *, batch, seq, lam_init):
    tq = 2 * FLASH_TK
    nq = seq // tq
    vec = pl.BlockSpec((1, B_DH), lambda b, h, i: (0, 0))
    return pl.pallas_call(
        functools.partial(_diff_attn_kernel, lam_init=lam_init),
        grid=(batch, HB, nq),
        in_specs=[
            pl.BlockSpec((tq, LANES), lambda b, h, i: (b * nq + i, h)),
            pl.BlockSpec((seq, LANES), lambda b, h, i: (b, HB + h)),
            pl.BlockSpec((seq, LANES), lambda b, h, i: (b, 2 * HB + h)),
            vec, vec, vec, vec,
            pl.BlockSpec((1, 2 * B_DH), lambda b, h, i: (0, 0)),
        ],
        out_specs=pl.BlockSpec((tq, LANES), lambda b, h, i: (b * nq + i, h)),
        out_shape=jax.ShapeDtypeStruct((batch * seq, HB * 2 * B_DH), BF16),
        scratch_shapes=[pltpu.VMEM((seq, 2 * LANES), BF16), pltpu.VMEM((seq, 2 * LANES), BF16),
                        pltpu.VMEM((FLASH_TK, FLASH_TK), F32),
                        pltpu.VMEM((2 * tq, 2 * LANES), BF16)] + _flash_scratch(2 * tq),
        compiler_params=_params(3),
        name="diff_attn",
    )(proj, proj, proj, lq1, lk1, lq2, lk2, subln)


def _band_attn_kernel(q_ref, k_ref, v_ref, base_ref, o_ref, bias_ref, s_ref, *, gq, sub):
    b = pl.program_id(1)
    g2 = pl.program_id(2)
    win = 2 * gq

    @pl.when(jnp.logical_and(b == 0, g2 == 0))
    def _():
        x = jnp.broadcast_to(base_ref[0], (gq, win))
        x = pltpu.roll(x, 0, 1, stride=1, stride_axis=0)
        d = _chunk_ids((gq, win), 1) - _chunk_ids((gq, win), 0)
        bias_ref[...] = jnp.where(jnp.logical_and(d >= 0, d <= LEFT_CHUNKS), x * LOG2E, NEG)

    def attend(half, keys, bias_cols):
        width = bias_cols.stop - bias_cols.start
        q = q_ref[half * gq:(half + 1) * gq, :].astype(F32) * (C_DH ** -0.5 * LOG2E)
        s_ref[half, :, :width] = lax.dot_general(q.astype(BF16), k_ref[keys, :], NT_DIMS,
                                                 preferred_element_type=F32)
        v = v_ref[keys, :]
        for r0 in range(0, gq, sub):
            s = s_ref[half, r0:r0 + sub, :width] + bias_ref[r0:r0 + sub, bias_cols]
            p = jnp.exp2(s - jnp.max(s, axis=-1, keepdims=True))
            l = jnp.sum(p, axis=-1, keepdims=True)
            o = jnp.dot(p.astype(BF16), v, preferred_element_type=F32)
            o_ref[half * gq + r0:half * gq + r0 + sub, :] = (o / l).astype(BF16)

    @pl.when(g2 == 0)
    def _():
        attend(0, slice(0, gq), slice(gq, win))
        attend(1, slice(0, win), slice(0, win))

    @pl.when(g2 > 0)
    def _():
        off0 = pl.multiple_of((2 * g2 - 1) * gq, gq)
        off1 = pl.multiple_of(2 * g2 * gq, gq)
        attend(0, pl.ds(off0, win), slice(0, win))
        attend(1, pl.ds(off1, win), slice(0, win))


def _band_attn(proj, base, *, batch, seq):
    gq = LEFT_CHUNKS * CHUNK
    ng = seq // (2 * gq)
    qc, kc, vc = 3 * HB, 3 * HB + HC, 3 * HB + 2 * HC
    return pl.pallas_call(
        functools.partial(_band_attn_kernel, gq=gq, sub=256),
        grid=(HC, batch, ng),
        in_specs=[
            pl.BlockSpec((2 * gq, C_DH), lambda h, b, g: (b * ng + g, qc + h)),
            pl.BlockSpec((seq, C_DH), lambda h, b, g: (b, kc + h)),
            pl.BlockSpec((seq, C_DH), lambda h, b, g: (b, vc + h)),
            pl.BlockSpec((1, 1, 2 * gq), lambda h, b, g: (h, 0, 0)),
        ],
        out_specs=pl.BlockSpec((2 * gq, C_DH), lambda h, b, g: (b * ng + g, h)),
        out_shape=jax.ShapeDtypeStruct((batch * seq, HC * C_DH), BF16),
        scratch_shapes=[pltpu.VMEM((gq, 2 * gq), F32), pltpu.VMEM((2, gq, 2 * gq), F32)],
        compiler_params=_params(3),
        name="band_attn",
    )(proj, proj, proj, base)


def _oproj_kernel(a_ref, b_ref, c_ref, x_ref, w_ref, o_ref):
    na, nb = a_ref.shape[1], a_ref.shape[1] + b_ref.shape[1]
    acc = jnp.dot(a_ref[...], w_ref[:na, :], preferred_element_type=F32)
    acc += jnp.dot(b_ref[...], w_ref[na:nb, :], preferred_element_type=F32)
    acc += jnp.dot(c_ref[...], w_ref[nb:, :], preferred_element_type=F32)
    o_ref[...] = x_ref[...] + acc


def _oproj(a, b, c, x, w, layer, *, tm=512):
    t, d = x.shape
    row = lambda i: (i, 0)
    return pl.pallas_call(
        _oproj_kernel,
        grid=(t // tm,),
        in_specs=[
            pl.BlockSpec((tm, a.shape[1]), row),
            pl.BlockSpec((tm, b.shape[1]), row),
            pl.BlockSpec((tm, c.shape[1]), row),
            pl.BlockSpec((tm, d), row),
            pl.BlockSpec((None,) + w.shape[1:], lambda i: (layer, 0, 0)),
        ],
        out_specs=pl.BlockSpec((tm, d), row),
        out_shape=jax.ShapeDtypeStruct((t, d), F32),
        compiler_params=_params(1),
        name="oproj",
    )(a, b, c, x, w)


def _mlp_kernel(x_ref, g_ref, wu_ref, wd_ref, gf_ref, o_ref, xn_ref, *, rows, final_norm):
    f = pl.program_id(1)

    @pl.when(f == 0)
    def _():
        for r in range(0, x_ref.shape[0], rows):
            xr = x_ref[r:r + rows, :]
            xn_ref[r:r + rows, :] = _rms(xr, g_ref[...]).astype(BF16)
            o_ref[r:r + rows, :] = xr

    hidden = jnp.dot(xn_ref[...], wu_ref[...], preferred_element_type=F32)
    hidden = jnp.square(jnp.maximum(hidden, 0.0)).astype(BF16)
    o_ref[...] += jnp.dot(hidden, wd_ref[...], preferred_element_type=F32)

    if final_norm:
        @pl.when(f == pl.num_programs(1) - 1)
        def _():
            for r in range(0, x_ref.shape[0], rows):
                o_ref[r:r + rows, :] = _rms(o_ref[r:r + rows, :], gf_ref[...])


def _mlp(x, g, wu, wd, gf, layer, *, final_norm, tm=1024):
    t, d = x.shape
    nf, tf = wu.shape[1], wu.shape[3]
    return pl.pallas_call(
        functools.partial(_mlp_kernel, rows=256, final_norm=final_norm),
        grid=(t // tm, nf),
        in_specs=[
            pl.BlockSpec((tm, d), lambda i, f: (i, 0)),
            pl.BlockSpec((1, d), lambda i, f: (0, 0)),
            pl.BlockSpec((None, None, d, tf), lambda i, f: (layer, f, 0, 0)),
            pl.BlockSpec((None, tf, d), lambda i, f: (layer, f, 0)),
            pl.BlockSpec((1, d), lambda i, f: (0, 0)),
        ],
        out_specs=pl.BlockSpec((tm, d), lambda i, f: (i, 0)),
        out_shape=jax.ShapeDtypeStruct((t, d), F32),
        scratch_shapes=[pltpu.VMEM((tm, d), BF16)],
        compiler_params=_params(2),
        name="mlp",
    )(x, g, wu, wd, gf)


def _rope_tables(seq):
    half = A_ROPE // 2
    lane = jnp.arange(LANES)
    inv_freq = ROPE_THETA ** (-(lane % half).astype(F32) / half)
    ang = jnp.arange(seq, dtype=F32)[:, None] * inv_freq[None, :]
    cos, sin = jnp.cos(ang), jnp.sin(ang)
    return jnp.concatenate([
        jnp.where(lane < A_ROPE, cos, 0.0),
        jnp.where((lane >= half) & (lane < A_ROPE), sin, 0.0),
        jnp.where(lane < half, -sin, 0.0)], axis=1)


def _split_in_weight(w_in):
    lat = Q_RANK + KV_RANK + A_ROPE
    w_lat = jnp.pad(w_in[:, :lat], ((0, 0), (0, LAT_W - lat))).astype(BF16)
    return w_lat, w_in[:, lat:].astype(BF16)


def _band_base(rel_bias):
    gq = LEFT_CHUNKS * CHUNK
    edge = rel_bias[:, REL_SIZE - 1:]
    head = jnp.broadcast_to(edge, (HC, gq - REL_CLIP))
    tail = jnp.broadcast_to(edge, (HC, 2 * gq - (gq - REL_CLIP) - REL_SIZE))
    return jnp.concatenate([head, rel_bias[:, ::-1], tail], axis=1).reshape(HC, 1, 2 * gq)


def kernel(x, attn_norm, w_in, q_a_norm, kv_a_norm, w_uq, w_ukv, lambda_q1, lambda_k1,
           lambda_q2, lambda_k2, diff_subln, rel_bias, w_o, mlp_norm, w_up, w_down, final_norm):
    batch, seq, d = x.shape
    xf = x.reshape(batch * seq, d)
    rope = _rope_tables(seq)
    row = lambda v: v.reshape(1, -1)
    wo_all, wd_all = w_o.astype(BF16), w_down.astype(BF16)
    wu_all = w_up.astype(BF16).reshape(DEPTH, d, D_FF // MLP_TF, MLP_TF).transpose(0, 2, 1, 3)
    for l in range(DEPTH):
        lat, proj = _inproj(xf, row(attn_norm[l]), *_split_in_weight(w_in[l]))

        wuq = w_uq[l].reshape(Q_RANK, HA, A_NOPE + A_ROPE)
        wuq = jnp.pad(wuq, ((0, 0), (0, 0), (0, A_QK - A_NOPE - A_ROPE)))
        wuq = wuq.reshape(Q_RANK, HA * A_QK).astype(BF16)
        wukv = w_ukv[l].reshape(KV_RANK, HA, A_NOPE + A_V)
        wukt = wukv[:, :, :A_NOPE].transpose(1, 2, 0).astype(BF16)
        wuv = wukv[:, :, A_NOPE:].transpose(1, 0, 2).astype(BF16)
        q, k, v = _mla_prep(lat, row(q_a_norm[l]), row(kv_a_norm[l]), wuq, wukt, rope, seq=seq)
        out_a = _mla_attn(q, k, v, wuv, batch=batch, seq=seq)

        lam_init = 0.8 - 0.6 * math.exp(-0.3 * l)
        out_b = _diff_attn(proj, row(lambda_q1[l]), row(lambda_k1[l]), row(lambda_q2[l]),
                           row(lambda_k2[l]), row(diff_subln[l]),
                           batch=batch, seq=seq, lam_init=lam_init)
        out_c = _band_attn(proj, _band_base(rel_bias[l]), batch=batch, seq=seq)

        xf = _oproj(out_a, out_b, out_c, xf, wo_all, l)
        xf = _mlp(xf, row(mlp_norm[l]), wu_all, wd_all, row(final_norm), l,
                  final_norm=(l == DEPTH - 1))
    return xf.reshape(batch, seq, d)
```

```python
import functools
import math

import jax
import jax.numpy as jnp
from jax import lax
from jax.experimental import pallas as pl
from jax.experimental.pallas import tpu as pltpu

F32 = jnp.float32
BF16 = jnp.bfloat16

D_MODEL = 2048
DEPTH = 2
CHUNK = 64
CHUNK_SHIFT = CHUNK.bit_length() - 1
HA, A_NOPE, A_ROPE, A_V = 8, 128, 64, 128
Q_RANK, KV_RANK = 384, 128
ROPE_THETA = 10000.0
HB, B_DH = 4, 64
HC, C_DH = 4, 128
LEFT_CHUNKS = 8
REL_CLIP = 256
REL_SIZE = REL_CLIP + CHUNK
D_FF = 4 * D_MODEL
EPS = 1e-6
NEG = -1e30

LANES = 128
LAT_W = 768
PROJ_W = 6 * 512
A_QK = 256
MLA_TQ = 256
FLASH_TK = 512
MLP_TF = 512
LOG2E = math.log2(math.e)
VMEM_LIMIT = 56 * 1024 * 1024

NT_DIMS = (((1,), (1,)), ((), ()))


def _params(n_axes, vmem=VMEM_LIMIT):
    return pltpu.CompilerParams(
        dimension_semantics=("arbitrary",) * n_axes, vmem_limit_bytes=vmem)


def _rms(x, g):
    return x * lax.rsqrt(jnp.mean(x * x, axis=-1, keepdims=True) + EPS) * g


def _chunk_ids(shape, dim):
    return lax.broadcasted_iota(jnp.int32, shape, dim) >> CHUNK_SHIFT


def _inproj_kernel(x_ref, g_ref, wl_ref, wp_ref, lat_ref, proj_ref, xn_ref, *, rows):
    for r in range(0, x_ref.shape[0], rows):
        xn_ref[r:r + rows, :] = _rms(x_ref[r:r + rows, :], g_ref[...]).astype(BF16)
    lat_ref[...] = jnp.dot(xn_ref[...], wl_ref[...], preferred_element_type=F32)
    for c in range(0, proj_ref.shape[1], LAT_W):
        proj_ref[:, c:c + LAT_W] = jnp.dot(xn_ref[...], wp_ref[:, c:c + LAT_W],
                                          preferred_element_type=F32).astype(BF16)


def _inproj(x, g, w_lat, w_proj, *, tm=512):
    t, d = x.shape
    const = lambda i: (0, 0)
    return pl.pallas_call(
        functools.partial(_inproj_kernel, rows=256),
        grid=(t // tm,),
        in_specs=[
            pl.BlockSpec((tm, d), lambda i: (i, 0)),
            pl.BlockSpec((1, d), const),
            pl.BlockSpec(w_lat.shape, const),
            pl.BlockSpec(w_proj.shape, const),
        ],
        out_specs=[
            pl.BlockSpec((tm, LAT_W), lambda i: (i, 0)),
            pl.BlockSpec((tm, PROJ_W), lambda i: (i, 0)),
        ],
        out_shape=[
            jax.ShapeDtypeStruct((t, LAT_W), F32),
            jax.ShapeDtypeStruct((t, PROJ_W), BF16),
        ],
        scratch_shapes=[pltpu.VMEM((tm, d), BF16)],
        compiler_params=_params(1),
        name="inproj",
    )(x, g, w_lat, w_proj)


def _mla_prep_kernel(lat_ref, qn_ref, kvn_ref, wuq_ref, wukt_ref, rope_ref,
                     q_ref, k_ref, v_ref):
    tm = lat_ref.shape[0]
    cqn = _rms(lat_ref[:, :Q_RANK], qn_ref[...]).astype(BF16)
    ckvn = _rms(lat_ref[:, Q_RANK:Q_RANK + KV_RANK], kvn_ref[...]).astype(BF16)
    kr = lat_ref[:, Q_RANK + KV_RANK:Q_RANK + KV_RANK + LANES]
    cos = rope_ref[:, :LANES]
    sin = rope_ref[:, LANES:]

    def rot(r):
        return r * cos + pltpu.roll(r, LANES // 2, 1) * sin

    scale = (A_NOPE + A_ROPE) ** -0.5 * LOG2E
    q = jnp.dot(cqn, wuq_ref[...], preferred_element_type=F32)
    for h in range(HA):
        c0 = A_QK * h
        qa = jnp.dot(q[:, c0:c0 + LANES].astype(BF16), wukt_ref[h],
                     preferred_element_type=F32)
        qa = (qa * scale).astype(BF16)
        qr = (rot(q[:, c0 + LANES:c0 + A_QK]) * scale).astype(BF16)
        for blk in range(tm // MLA_TQ):
            r0 = (blk * HA + h) * MLA_TQ
            q_ref[r0:r0 + MLA_TQ, :LANES] = qa[blk * MLA_TQ:(blk + 1) * MLA_TQ]
            q_ref[r0:r0 + MLA_TQ, LANES:] = qr[blk * MLA_TQ:(blk + 1) * MLA_TQ]
    k_ref[:, :LANES] = ckvn
    k_ref[:, LANES:] = rot(kr).astype(BF16)
    v_ref[:, :LANES] = ckvn
    v_ref[:, LANES:] = jnp.ones((tm, LANES), BF16)


def _mla_prep(lat, qn, kvn, wuq, wukt, rope, *, seq, tm=1024):
    t = lat.shape[0]
    const = lambda i: (0, 0)
    return pl.pallas_call(
        _mla_prep_kernel,
        grid=(t // tm,),
        in_specs=[
            pl.BlockSpec((tm, LAT_W), lambda i: (i, 0)),
            pl.BlockSpec((1, Q_RANK), const),
            pl.BlockSpec((1, KV_RANK), const),
            pl.BlockSpec(wuq.shape, const),
            pl.BlockSpec(wukt.shape, lambda i: (0, 0, 0)),
            pl.BlockSpec((tm, 2 * LANES), lambda i: (i % (seq // tm), 0)),
        ],
        out_specs=[
            pl.BlockSpec((tm * HA, A_QK), lambda i: (i, 0)),
            pl.BlockSpec((tm, A_QK), lambda i: (i, 0)),
            pl.BlockSpec((tm, A_QK), lambda i: (i, 0)),
        ],
        out_shape=[
            jax.ShapeDtypeStruct((t * HA, A_QK), BF16),
            jax.ShapeDtypeStruct((t, A_QK), BF16),
            jax.ShapeDtypeStruct((t, A_QK), BF16),
        ],
        compiler_params=_params(1),
        name="mla_prep",
    )(lat, qn, kvn, wuq, wukt, rope)


def _flash_ops(q_ref, k_ref, v_ref, sa_ref, sb_ref, m_ref, acc_ref, exp_fn):
    tk = sa_ref.shape[1]
    everything = slice(None)

    def qk(t, dst_ref, rows=everything):
        off = pl.multiple_of(t * tk, tk)
        dst_ref[rows, :] = lax.dot_general(q_ref[rows, :], k_ref[pl.ds(off, tk), :], NT_DIMS,
                                           preferred_element_type=F32)

    def softmax_pv(t, src_ref, bias=None, rows=everything):
        s = src_ref[rows, :]
        if bias is not None:
            nb = bias.shape[0]
            s = jnp.concatenate([s[:nb] + bias, s[nb:]], axis=0) if nb < s.shape[0] else s + bias
        m_prev = m_ref[rows, :]
        m_next = jnp.maximum(m_prev, jnp.max(s, axis=1)[:, None])
        p = exp_fn(s - jnp.tile(m_next, (1, tk // LANES)))
        alpha = exp_fn(m_prev - m_next)
        off = pl.multiple_of(t * tk, tk)
        pv = jnp.dot(p.astype(BF16), v_ref[pl.ds(off, tk), :], preferred_element_type=F32)
        acc_ref[rows, :] = jnp.tile(alpha, (1, 2)) * acc_ref[rows, :] + pv
        m_ref[rows, :] = m_next

    def pair(t):
        qk(t + 1, sb_ref)
        softmax_pv(t, sa_ref)
        qk(t + 2, sa_ref)
        softmax_pv(t + 1, sb_ref)

    def start_and_full_pairs(npairs):
        m_ref[...] = jnp.full(m_ref.shape, NEG, F32)
        acc_ref[...] = jnp.zeros(acc_ref.shape, F32)
        qk(0, sa_ref)

        def quad(j, carry):
            pair(4 * j)
            pair(4 * j + 2)
            return carry

        lax.fori_loop(0, npairs >> 1, quad, 0)

        @pl.when((npairs & 1) != 0)
        def _():
            pair(2 * (npairs - 1))

    return qk, softmax_pv, start_and_full_pairs


def _flash_scratch(rows):
    return [pltpu.VMEM((rows, FLASH_TK), F32), pltpu.VMEM((rows, FLASH_TK), F32),
            pltpu.VMEM((rows, LANES), F32), pltpu.VMEM((rows, 2 * LANES), F32)]


def _mla_attn_kernel(q_ref, k_ref, v_ref, wuv_ref, o_ref, sa_ref, sb_ref, m_ref, acc_ref):
    i = pl.program_id(1)
    n = (i * MLA_TQ) // FLASH_TK

    def last_bias():
        shape = (MLA_TQ, FLASH_TK)
        rows = i * MLA_TQ + lax.broadcasted_iota(jnp.int32, shape, 0)
        cols = n * FLASH_TK + lax.broadcasted_iota(jnp.int32, shape, 1)
        bias = jnp.where((cols >> CHUNK_SHIFT) <= (rows >> CHUNK_SHIFT), 0.0, NEG)
        return jnp.tile(bias, (HA, 1))

    qk, softmax_pv, start_and_full_pairs = _flash_ops(
        q_ref, k_ref, v_ref, sa_ref, sb_ref, m_ref, acc_ref, jnp.exp2)
    start_and_full_pairs(n >> 1)

    @pl.when((n & 1) == 0)
    def _():
        softmax_pv(n, sa_ref, last_bias())

    @pl.when((n & 1) == 1)
    def _():
        qk(n, sb_ref)
        softmax_pv(n - 1, sa_ref)
        softmax_pv(n, sb_ref, last_bias())

    for h in range(HA):
        a = acc_ref[h * MLA_TQ:(h + 1) * MLA_TQ, :]
        o = (a[:, :LANES] / a[:, LANES:]).astype(BF16)
        o_ref[:, h * A_V:(h + 1) * A_V] = jnp.dot(
            o, wuv_ref[h], preferred_element_type=F32).astype(BF16)


def _mla_attn(q, k, v, wuv, *, batch, seq):
    nq = seq // MLA_TQ
    rows = HA * MLA_TQ
    return pl.pallas_call(
        _mla_attn_kernel,
        grid=(batch, nq),
        in_specs=[
            pl.BlockSpec((rows, A_QK), lambda b, i: (b * nq + i, 0)),
            pl.BlockSpec((seq, A_QK), lambda b, i: (b, 0)),
            pl.BlockSpec((seq, A_QK), lambda b, i: (b, 0)),
            pl.BlockSpec(wuv.shape, lambda b, i: (0, 0, 0)),
        ],
        out_specs=pl.BlockSpec((MLA_TQ, HA * A_V), lambda b, i: (b * nq + i, 0)),
        out_shape=jax.ShapeDtypeStruct((batch * seq, HA * A_V), BF16),
        scratch_shapes=_flash_scratch(rows),
        compiler_params=_params(2),
        name="mla_attn",
    )(q, k, v, wuv)


def _diff_attn_kernel(q_ref, k_ref, v_ref, lq1_ref, lk1_ref, lq2_ref, lk2_ref, g_ref, o_ref,
                      kaug_ref, vaug_ref, dbias_ref, qs_ref, sa_ref, sb_ref, m_ref, acc_ref,
                      *, lam_init):
    h = pl.program_id(1)
    i = pl.program_id(2)
    tq = FLASH_TK
    lane = lax.broadcasted_iota(jnp.int32, (tq, LANES), 1)
    slope_bits = (127 - (8 // HB) * (h + 1)) << 23
    slope = lax.bitcast_convert_type(jnp.full((1, LANES), slope_bits, jnp.int32), F32)

    @pl.when(i == 0)
    def _():
        r = lax.broadcasted_iota(jnp.int32, (tq, tq), 0)
        c = lax.broadcasted_iota(jnp.int32, (tq, tq), 1)
        ahead = jnp.maximum(c - r, 0).astype(F32)
        bias = jnp.tile(slope, (1, tq // LANES)) * (-2.0 * ahead)
        dbias_ref[...] = jnp.where((c >> CHUNK_SHIFT) <= (r >> CHUNK_SHIFT), bias, NEG)

        def fill(c, carry):
            off = pl.multiple_of(c * tq, tq)
            pos = off + lax.broadcasted_iota(jnp.int32, (tq, LANES), 0)
            cols = jnp.where(lane == 0, pos >> CHUNK_SHIFT,
                             jnp.where(lane == 1, pos & (CHUNK - 1), 0))
            kaug_ref[pl.ds(off, tq), :LANES] = k_ref[pl.ds(off, tq), :]
            kaug_ref[pl.ds(off, tq), LANES:] = cols.astype(F32).astype(BF16)
            vaug_ref[pl.ds(off, tq), :LANES] = v_ref[pl.ds(off, tq), :]
            vaug_ref[pl.ds(off, tq), LANES:] = jnp.ones((tq, LANES), BF16)
            return carry

        lax.fori_loop(0, k_ref.shape[0] // tq, fill, 0)

    pos_cols = jnp.where(lane == 0, slope * CHUNK, jnp.where(lane == 1, slope, 0.0)).astype(BF16)
    for half in range(2):
        q = q_ref[half * tq:(half + 1) * tq, :].astype(F32) * (B_DH ** -0.5)
        r0 = 2 * half * tq
        qs_ref[r0:r0 + tq, :LANES] = jnp.where(lane < B_DH, q, 0.0).astype(BF16)
        qs_ref[r0 + tq:r0 + 2 * tq, :LANES] = jnp.where(lane >= B_DH, q, 0.0).astype(BF16)
        qs_ref[r0:r0 + tq, LANES:] = pos_cols
        qs_ref[r0 + tq:r0 + 2 * tq, LANES:] = pos_cols

    qk, softmax_pv, start_and_full_pairs = _flash_ops(
        qs_ref, kaug_ref, vaug_ref, sa_ref, sb_ref, m_ref, acc_ref, jnp.exp)
    start_and_full_pairs(i)
    bottom = slice(2 * tq, 4 * tq)
    diag_bias = jnp.tile(dbias_ref[...], (2, 1))
    qk(2 * i + 1, sb_ref, rows=bottom)
    softmax_pv(2 * i, sa_ref, diag_bias)
    softmax_pv(2 * i + 1, sb_ref, diag_bias, rows=bottom)

    lam = (jnp.exp(jnp.sum(lq1_ref[...] * lk1_ref[...], axis=-1, keepdims=True))
           - jnp.exp(jnp.sum(lq2_ref[...] * lk2_ref[...], axis=-1, keepdims=True))
           + lam_init)
    for half in range(2):
        r0 = 2 * half * tq
        o1 = acc_ref[r0:r0 + tq, :LANES] / acc_ref[r0:r0 + tq, LANES:]
        o2 = acc_ref[r0 + tq:r0 + 2 * tq, :LANES] / acc_ref[r0 + tq:r0 + 2 * tq, LANES:]
        o = o1 - lam * o2
        o_ref[half * tq:(half + 1) * tq, :] = (
            _rms(o, g_ref[...]) * (1.0 - lam_init)).astype(BF16)


def _diff_attn(proj, lq1, lk1, lq2, lk2, subln, *, batch, seq, lam_init):
    tq = 2 * FLASH_TK
    nq = seq // tq
    vec = pl.BlockSpec((1, B_DH), lambda b, h, i: (0, 0))
    return pl.pallas_call(
        functools.partial(_diff_attn_kernel, lam_init=lam_init),
        grid=(batch, HB, nq),
        in_specs=[
            pl.BlockSpec((tq, LANES), lambda b, h, i: (b * nq + i, h)),
            pl.BlockSpec((seq, LANES), lambda b, h, i: (b, HB + h)),
            pl.BlockSpec((seq, LANES), lambda b, h, i: (b, 2 * HB + h)),
            vec, vec, vec, vec,
            pl.BlockSpec((1, 2 * B_DH), lambda b, h, i: (0, 0)),
        ],
        out_specs=pl.BlockSpec((tq, LANES), lambda b, h, i: (b * nq + i, h)),
        out_shape=jax.ShapeDtypeStruct((batch * seq, HB * 2 * B_DH), BF16),
        scratch_shapes=[pltpu.VMEM((seq, 2 * LANES), BF16), pltpu.VMEM((seq, 2 * LANES), BF16),
                        pltpu.VMEM((FLASH_TK, FLASH_TK), F32),
                        pltpu.VMEM((2 * tq, 2 * LANES), BF16)] + _flash_scratch(2 * tq),
        compiler_params=_params(3),
        name="diff_attn",
    )(proj, proj, proj, lq1, lk1, lq2, lk2, subln)


def _band_attn_kernel(q_ref, k_ref, v_ref, base_ref, o_ref, bias_ref, s_ref, *, gq, sub):
    b = pl.program_id(1)
    g2 = pl.program_id(2)
    win = 2 * gq

    @pl.when(jnp.logical_and(b == 0, g2 == 0))
    def _():
        x = jnp.broadcast_to(base_ref[0], (gq, win))
        x = pltpu.roll(x, 0, 1, stride=1, stride_axis=0)
        d = _chunk_ids((gq, win), 1) - _chunk_ids((gq, win), 0)
        bias_ref[...] = jnp.where(jnp.logical_and(d >= 0, d <= LEFT_CHUNKS), x * LOG2E, NEG)

    def attend(half, keys, bias_cols):
        width = bias_cols.stop - bias_cols.start
        q = q_ref[half * gq:(half + 1) * gq, :].astype(F32) * (C_DH ** -0.5 * LOG2E)
        s_ref[half, :, :width] = lax.dot_general(q.astype(BF16), k_ref[keys, :], NT_DIMS,
                                                 preferred_element_type=F32)
        v = v_ref[keys, :]
        for r0 in range(0, gq, sub):
            s = s_ref[half, r0:r0 + sub, :width] + bias_ref[r0:r0 + sub, bias_cols]
            p = jnp.exp2(s - jnp.max(s, axis=-1, keepdims=True))
            l = jnp.sum(p, axis=-1, keepdims=True)
            o = jnp.dot(p.astype(BF16), v, preferred_element_type=F32)
            o_ref[half * gq + r0:half * gq + r0 + sub, :] = (o / l).astype(BF16)

    @pl.when(g2 == 0)
    def _():
        attend(0, slice(0, gq), slice(gq, win))
        attend(1, slice(0, win), slice(0, win))

    @pl.when(g2 > 0)
    def _():
        off0 = pl.multiple_of((2 * g2 - 1) * gq, gq)
        off1 = pl.multiple_of(2 * g2 * gq, gq)
        attend(0, pl.ds(off0, win), slice(0, win))
        attend(1, pl.ds(off1, win), slice(0, win))


def _band_attn(proj, base, *, batch, seq):
    gq = LEFT_CHUNKS * CHUNK
    ng = seq // (2 * gq)
    qc, kc, vc = 3 * HB, 3 * HB + HC, 3 * HB + 2 * HC
    return pl.pallas_call(
        functools.partial(_band_attn_kernel, gq=gq, sub=256),
        grid=(HC, batch, ng),
        in_specs=[
            pl.BlockSpec((2 * gq, C_DH), lambda h, b, g: (b * ng + g, qc + h)),
            pl.BlockSpec((seq, C_DH), lambda h, b, g: (b, kc + h)),
            pl.BlockSpec((seq, C_DH), lambda h, b, g: (b, vc + h)),
            pl.BlockSpec((1, 1, 2 * gq), lambda h, b, g: (h, 0, 0)),
        ],
        out_specs=pl.BlockSpec((2 * gq, C_DH), lambda h, b, g: (b * ng + g, h)),
        out_shape=jax.ShapeDtypeStruct((batch * seq, HC * C_DH), BF16),
        scratch_shapes=[pltpu.VMEM((gq, 2 * gq), F32), pltpu.VMEM((2, gq, 2 * gq), F32)],
        compiler_params=_params(3),
        name="band_attn",
    )(proj, proj, proj, base)


def _oproj_kernel(a_ref, b_ref, c_ref, x_ref, w_ref, o_ref):
    na, nb = a_ref.shape[1], a_ref.shape[1] + b_ref.shape[1]
    acc = jnp.dot(a_ref[...], w_ref[:na, :], preferred_element_type=F32)
    acc += jnp.dot(b_ref[...], w_ref[na:nb, :], preferred_element_type=F32)
    acc += jnp.dot(c_ref[...], w_ref[nb:, :], preferred_element_type=F32)
    o_ref[...] = x_ref[...] + acc


def _oproj(a, b, c, x, w, layer, *, tm=512):
    t, d = x.shape
    row = lambda i: (i, 0)
    return pl.pallas_call(
        _oproj_kernel,
        grid=(t // tm,),
        in_specs=[
            pl.BlockSpec((tm, a.shape[1]), row),
            pl.BlockSpec((tm, b.shape[1]), row),
            pl.BlockSpec((tm, c.shape[1]), row),
            pl.BlockSpec((tm, d), row),
            pl.BlockSpec((None,) + w.shape[1:], lambda i: (layer, 0, 0)),
        ],
        out_specs=pl.BlockSpec((tm, d), row),
        out_shape=jax.ShapeDtypeStruct((t, d), F32),
        compiler_params=_params(1),
        name="oproj",
    )(a, b, c, x, w)


def _mlp_kernel(x_ref, g_ref, wu_ref, wd_ref, gf_ref, o_ref, xn_ref, *, rows, final_norm):
    f = pl.program_id(1)

    @pl.when(f == 0)
    def _():
        for r in range(0, x_ref.shape[0], rows):
            xr = x_ref[r:r + rows, :]
            xn_ref[r:r + rows, :] = _rms(xr, g_ref[...]).astype(BF16)
            o_ref[r:r + rows, :] = xr

    hidden = jnp.dot(xn_ref[...], wu_ref[...], preferred_element_type=F32)
    hidden = jnp.square(jnp.maximum(hidden, 0.0)).astype(BF16)
    o_ref[...] += jnp.dot(hidden, wd_ref[...], preferred_element_type=F32)

    if final_norm:
        @pl.when(f == pl.num_programs(1) - 1)
        def _():
            for r in range(0, x_ref.shape[0], rows):
                o_ref[r:r + rows, :] = _rms(o_ref[r:r + rows, :], gf_ref[...])


def _mlp(x, g, wu, wd, gf, layer, *, final_norm, tm=1024, tf=MLP_TF):
    t, d = x.shape
    return pl.pallas_call(
        functools.partial(_mlp_kernel, rows=256, final_norm=final_norm),
        grid=(t // tm, wu.shape[2] // tf),
        in_specs=[
            pl.BlockSpec((tm, d), lambda i, f: (i, 0)),
            pl.BlockSpec((1, d), lambda i, f: (0, 0)),
            pl.BlockSpec((None, d, tf), lambda i, f: (layer, 0, f)),
            pl.BlockSpec((None, tf, d), lambda i, f: (layer, f, 0)),
            pl.BlockSpec((1, d), lambda i, f: (0, 0)),
        ],
        out_specs=pl.BlockSpec((tm, d), lambda i, f: (i, 0)),
        out_shape=jax.ShapeDtypeStruct((t, d), F32),
        scratch_shapes=[pltpu.VMEM((tm, d), BF16)],
        compiler_params=_params(2),
        name="mlp",
    )(x, g, wu, wd, gf)


def _rope_tables(seq):
    half = A_ROPE // 2
    lane = jnp.arange(LANES)
    inv_freq = ROPE_THETA ** (-(lane % half).astype(F32) / half)
    ang = jnp.arange(seq, dtype=F32)[:, None] * inv_freq[None, :]
    cos, sin = jnp.cos(ang), jnp.sin(ang)
    used = (lane % (LANES // 2)) < half
    return jnp.concatenate([
        jnp.where(used, cos, 0.0),
        jnp.where(used, jnp.where(lane < LANES // 2, -sin, sin), 0.0)], axis=1)


def _spread_rope(w):
    half = A_ROPE // 2
    z = jnp.zeros(w.shape[:-1] + (LANES // 2 - half,), w.dtype)
    return jnp.concatenate([w[..., :half], z, w[..., half:], z], axis=-1)


def _split_in_weight(w_in):
    lat = Q_RANK + KV_RANK
    w_lat = jnp.concatenate([w_in[:, :lat], _spread_rope(w_in[:, lat:lat + A_ROPE])], axis=1)
    w_lat = jnp.pad(w_lat, ((0, 0), (0, LAT_W - lat - LANES))).astype(BF16)
    return w_lat, w_in[:, lat + A_ROPE:].astype(BF16)


def _band_base(rel_bias):
    gq = LEFT_CHUNKS * CHUNK
    edge = rel_bias[:, REL_SIZE - 1:]
    head = jnp.broadcast_to(edge, (HC, gq - REL_CLIP))
    tail = jnp.broadcast_to(edge, (HC, 2 * gq - (gq - REL_CLIP) - REL_SIZE))
    return jnp.concatenate([head, rel_bias[:, ::-1], tail], axis=1).reshape(HC, 1, 2 * gq)


def kernel(x, attn_norm, w_in, q_a_norm, kv_a_norm, w_uq, w_ukv, lambda_q1, lambda_k1,
           lambda_q2, lambda_k2, diff_subln, rel_bias, w_o, mlp_norm, w_up, w_down, final_norm):
    batch, seq, d = x.shape
    xf = x.reshape(batch * seq, d)
    rope = _rope_tables(seq)
    row = lambda v: v.reshape(1, -1)
    wo_all, wu_all, wd_all = w_o.astype(BF16), w_up.astype(BF16), w_down.astype(BF16)
    for l in range(DEPTH):
        lat, proj = _inproj(xf, row(attn_norm[l]), *_split_in_weight(w_in[l]))

        wuq = w_uq[l].reshape(Q_RANK, HA, A_NOPE + A_ROPE)
        wuq = jnp.concatenate([wuq[..., :A_NOPE], _spread_rope(wuq[..., A_NOPE:])], axis=-1)
        wuq = wuq.reshape(Q_RANK, HA * A_QK).astype(BF16)
        wukv = w_ukv[l].reshape(KV_RANK, HA, A_NOPE + A_V)
        wukt = wukv[:, :, :A_NOPE].transpose(1, 2, 0).astype(BF16)
        wuv = wukv[:, :, A_NOPE:].transpose(1, 0, 2).astype(BF16)
        q, k, v = _mla_prep(lat, row(q_a_norm[l]), row(kv_a_norm[l]), wuq, wukt, rope, seq=seq)
        out_a = _mla_attn(q, k, v, wuv, batch=batch, seq=seq)

        lam_init = 0.8 - 0.6 * math.exp(-0.3 * l)
        out_b = _diff_attn(proj, row(lambda_q1[l]), row(lambda_k1[l]), row(lambda_q2[l]),
                           row(lambda_k2[l]), row(diff_subln[l]),
                           batch=batch, seq=seq, lam_init=lam_init)
        out_c = _band_attn(proj, _band_base(rel_bias[l]), batch=batch, seq=seq)

        xf = _oproj(out_a, out_b, out_c, xf, wo_all, l)
        xf = _mlp(xf, row(mlp_norm[l]), wu_all, wd_all, row(final_norm), l,
                  final_norm=(l == DEPTH - 1))
    return xf.reshape(batch, seq, d)
```

```python
import functools
import math

import jax
import jax.numpy as jnp
from jax import lax
from jax.experimental import pallas as pl
from jax.experimental.pallas import tpu as pltpu

F32 = jnp.float32
BF16 = jnp.bfloat16

D_MODEL = 2048
DEPTH = 2
CHUNK = 64
CHUNK_SHIFT = CHUNK.bit_length() - 1
HA, A_NOPE, A_ROPE, A_V = 8, 128, 64, 128
Q_RANK, KV_RANK = 384, 128
ROPE_THETA = 10000.0
HB, B_DH = 4, 64
HC, C_DH = 4, 128
LEFT_CHUNKS = 8
REL_CLIP = 256
REL_SIZE = REL_CLIP + CHUNK
D_FF = 4 * D_MODEL
EPS = 1e-6
NEG = -1e30

LANES = 128
LAT_W = 768
PROJ_W = 6 * 512
A_QK = 256
MLA_TQ = 256
FLASH_TK = 512
MLP_TF = 512
LOG2E = math.log2(math.e)
VMEM_LIMIT = 56 * 1024 * 1024

NT_DIMS = (((1,), (1,)), ((), ()))


def _params(n_axes, vmem=VMEM_LIMIT):
    return pltpu.CompilerParams(
        dimension_semantics=("arbitrary",) * n_axes, vmem_limit_bytes=vmem)


def _rms(x, g):
    return x * lax.rsqrt(jnp.mean(x * x, axis=-1, keepdims=True) + EPS) * g


def _chunk_ids(shape, dim):
    return lax.broadcasted_iota(jnp.int32, shape, dim) >> CHUNK_SHIFT


def _inproj_kernel(x_ref, g_ref, wl_ref, wp_ref, lat_ref, proj_ref, xn_ref, *, rows):
    for r in range(0, x_ref.shape[0], rows):
        xn_ref[r:r + rows, :] = _rms(x_ref[r:r + rows, :], g_ref[...]).astype(BF16)
        lat_ref[r:r + rows, :] = jnp.dot(xn_ref[r:r + rows, :], wl_ref[...],
                                         preferred_element_type=F32)
    for c in range(0, proj_ref.shape[1], LAT_W):
        proj_ref[:, c:c + LAT_W] = jnp.dot(xn_ref[...], wp_ref[:, c:c + LAT_W],
                                          preferred_element_type=F32).astype(BF16)


def _inproj(x, g, w_lat, w_proj, *, tm=512):
    t, d = x.shape
    const = lambda i: (0, 0)
    return pl.pallas_call(
        functools.partial(_inproj_kernel, rows=256),
        grid=(t // tm,),
        in_specs=[
            pl.BlockSpec((tm, d), lambda i: (i, 0)),
            pl.BlockSpec((1, d), const),
            pl.BlockSpec(w_lat.shape, const),
            pl.BlockSpec(w_proj.shape, const),
        ],
        out_specs=[
            pl.BlockSpec((tm, LAT_W), lambda i: (i, 0)),
            pl.BlockSpec((tm, PROJ_W), lambda i: (i, 0)),
        ],
        out_shape=[
            jax.ShapeDtypeStruct((t, LAT_W), F32),
            jax.ShapeDtypeStruct((t, PROJ_W), BF16),
        ],
        scratch_shapes=[pltpu.VMEM((tm, d), BF16)],
        compiler_params=_params(1),
        name="inproj",
    )(x, g, w_lat, w_proj)


def _mla_prep_kernel(lat_ref, qn_ref, kvn_ref, wuq_ref, wukt_ref, rope_ref,
                     q_ref, k_ref, v_ref):
    tm = lat_ref.shape[0]
    cqn = _rms(lat_ref[:, :Q_RANK], qn_ref[...]).astype(BF16)
    ckvn = _rms(lat_ref[:, Q_RANK:Q_RANK + KV_RANK], kvn_ref[...]).astype(BF16)
    kr = lat_ref[:, Q_RANK + KV_RANK:Q_RANK + KV_RANK + LANES]
    cos = rope_ref[:, :LANES]
    sin = rope_ref[:, LANES:]

    def rot(r):
        return r * cos + pltpu.roll(r, LANES // 2, 1) * sin

    scale = (A_NOPE + A_ROPE) ** -0.5 * LOG2E
    q = jnp.dot(cqn, wuq_ref[...], preferred_element_type=F32)
    for h in range(HA):
        c0 = A_QK * h
        qa = jnp.dot(q[:, c0:c0 + LANES].astype(BF16), wukt_ref[h],
                     preferred_element_type=F32)
        qa = (qa * scale).astype(BF16)
        qr = (rot(q[:, c0 + LANES:c0 + A_QK]) * scale).astype(BF16)
        for blk in range(tm // MLA_TQ):
            r0 = (blk * HA + h) * MLA_TQ
            q_ref[r0:r0 + MLA_TQ, :LANES] = qa[blk * MLA_TQ:(blk + 1) * MLA_TQ]
            q_ref[r0:r0 + MLA_TQ, LANES:] = qr[blk * MLA_TQ:(blk + 1) * MLA_TQ]
    k_ref[:, :LANES] = ckvn
    k_ref[:, LANES:] = rot(kr).astype(BF16)
    v_ref[:, :LANES] = ckvn
    v_ref[:, LANES:] = jnp.ones((tm, LANES), BF16)


def _mla_prep(lat, qn, kvn, wuq, wukt, rope, *, seq, tm=1024):
    t = lat.shape[0]
    const = lambda i: (0, 0)
    return pl.pallas_call(
        _mla_prep_kernel,
        grid=(t // tm,),
        in_specs=[
            pl.BlockSpec((tm, LAT_W), lambda i: (i, 0)),
            pl.BlockSpec((1, Q_RANK), const),
            pl.BlockSpec((1, KV_RANK), const),
            pl.BlockSpec(wuq.shape, const),
            pl.BlockSpec(wukt.shape, lambda i: (0, 0, 0)),
            pl.BlockSpec((tm, 2 * LANES), lambda i: (i % (seq // tm), 0)),
        ],
        out_specs=[
            pl.BlockSpec((tm * HA, A_QK), lambda i: (i, 0)),
            pl.BlockSpec((tm, A_QK), lambda i: (i, 0)),
            pl.BlockSpec((tm, A_QK), lambda i: (i, 0)),
        ],
        out_shape=[
            jax.ShapeDtypeStruct((t * HA, A_QK), BF16),
            jax.ShapeDtypeStruct((t, A_QK), BF16),
            jax.ShapeDtypeStruct((t, A_QK), BF16),
        ],
        compiler_params=_params(1),
        name="mla_prep",
    )(lat, qn, kvn, wuq, wukt, rope)


def _flash_ops(q_ref, k_ref, v_ref, sa_ref, sb_ref, m_ref, acc_ref, exp_fn):
    tk = sa_ref.shape[1]
    everything = slice(None)

    def qk(t, dst_ref, rows=everything):
        off = pl.multiple_of(t * tk, tk)
        dst_ref[rows, :] = lax.dot_general(q_ref[rows, :], k_ref[pl.ds(off, tk), :], NT_DIMS,
                                           preferred_element_type=F32)

    def softmax_pv(t, src_ref, bias=None, rows=everything):
        s = src_ref[rows, :]
        if bias is not None:
            nb = bias.shape[0]
            s = jnp.concatenate([s[:nb] + bias, s[nb:]], axis=0) if nb < s.shape[0] else s + bias
        m_prev = m_ref[rows, :]
        m_next = jnp.maximum(m_prev, jnp.max(s, axis=1)[:, None])
        p = exp_fn(s - jnp.tile(m_next, (1, tk // LANES)))
        alpha = exp_fn(m_prev - m_next)
        off = pl.multiple_of(t * tk, tk)
        pv = jnp.dot(p.astype(BF16), v_ref[pl.ds(off, tk), :], preferred_element_type=F32)
        acc_ref[rows, :] = jnp.tile(alpha, (1, 2)) * acc_ref[rows, :] + pv
        m_ref[rows, :] = m_next

    def pair(t):
        qk(t + 1, sb_ref)
        softmax_pv(t, sa_ref)
        qk(t + 2, sa_ref)
        softmax_pv(t + 1, sb_ref)

    def start_and_full_pairs(npairs):
        m_ref[...] = jnp.full(m_ref.shape, NEG, F32)
        acc_ref[...] = jnp.zeros(acc_ref.shape, F32)
        qk(0, sa_ref)

        def quad(j, carry):
            pair(4 * j)
            pair(4 * j + 2)
            return carry

        lax.fori_loop(0, npairs >> 1, quad, 0)

        @pl.when((npairs & 1) != 0)
        def _():
            pair(2 * (npairs - 1))

    return qk, softmax_pv, start_and_full_pairs


def _flash_scratch(rows):
    return [pltpu.VMEM((rows, FLASH_TK), F32), pltpu.VMEM((rows, FLASH_TK), F32),
            pltpu.VMEM((rows, LANES), F32), pltpu.VMEM((rows, 2 * LANES), F32)]


def _mla_attn_kernel(q_ref, k_ref, v_ref, wuv_ref, o_ref, sa_ref, sb_ref, m_ref, acc_ref):
    i = pl.program_id(1)
    n = (i * MLA_TQ) // FLASH_TK

    def last_bias():
        shape = (MLA_TQ, FLASH_TK)
        rows = i * MLA_TQ + lax.broadcasted_iota(jnp.int32, shape, 0)
        cols = n * FLASH_TK + lax.broadcasted_iota(jnp.int32, shape, 1)
        bias = jnp.where((cols >> CHUNK_SHIFT) <= (rows >> CHUNK_SHIFT), 0.0, NEG)
        return jnp.tile(bias, (HA, 1))

    qk, softmax_pv, start_and_full_pairs = _flash_ops(
        q_ref, k_ref, v_ref, sa_ref, sb_ref, m_ref, acc_ref, jnp.exp2)
    start_and_full_pairs(n >> 1)

    @pl.when((n & 1) == 0)
    def _():
        softmax_pv(n, sa_ref, last_bias())

    @pl.when((n & 1) == 1)
    def _():
        qk(n, sb_ref)
        softmax_pv(n - 1, sa_ref)
        softmax_pv(n, sb_ref, last_bias())

    for h in range(HA):
        a = acc_ref[h * MLA_TQ:(h + 1) * MLA_TQ, :]
        o = (a[:, :LANES] / a[:, LANES:]).astype(BF16)
        o_ref[:, h * A_V:(h + 1) * A_V] = jnp.dot(
            o, wuv_ref[h], preferred_element_type=F32).astype(BF16)


def _mla_attn(q, k, v, wuv, *, batch, seq):
    nq = seq // MLA_TQ
    rows = HA * MLA_TQ
    return pl.pallas_call(
        _mla_attn_kernel,
        grid=(batch, nq),
        in_specs=[
            pl.BlockSpec((rows, A_QK), lambda b, i: (b * nq + i, 0)),
            pl.BlockSpec((seq, A_QK), lambda b, i: (b, 0)),
            pl.BlockSpec((seq, A_QK), lambda b, i: (b, 0)),
            pl.BlockSpec(wuv.shape, lambda b, i: (0, 0, 0)),
        ],
        out_specs=pl.BlockSpec((MLA_TQ, HA * A_V), lambda b, i: (b * nq + i, 0)),
        out_shape=jax.ShapeDtypeStruct((batch * seq, HA * A_V), BF16),
        scratch_shapes=_flash_scratch(rows),
        compiler_params=_params(2),
        name="mla_attn",
    )(q, k, v, wuv)


def _diff_attn_kernel(q_ref, k_ref, v_ref, lq1_ref, lk1_ref, lq2_ref, lk2_ref, g_ref, o_ref,
                      kaug_ref, vaug_ref, dbias_ref, qs_ref, sa_ref, sb_ref, m_ref, acc_ref,
                      *, lam_init):
    h = pl.program_id(1)
    i = pl.program_id(2)
    tq = FLASH_TK
    lane = lax.broadcasted_iota(jnp.int32, (tq, LANES), 1)
    slope_bits = (127 - (8 // HB) * (h + 1)) << 23
    slope = lax.bitcast_convert_type(jnp.full((1, LANES), slope_bits, jnp.int32), F32)

    @pl.when(i == 0)
    def _():
        r = lax.broadcasted_iota(jnp.int32, (tq, tq), 0)
        c = lax.broadcasted_iota(jnp.int32, (tq, tq), 1)
        ahead = jnp.maximum(c - r, 0).astype(F32)
        bias = jnp.tile(slope, (1, tq // LANES)) * (-2.0 * ahead)
        dbias_ref[...] = jnp.where((c >> CHUNK_SHIFT) <= (r >> CHUNK_SHIFT), bias, NEG)

        def fill(c, carry):
            off = pl.multiple_of(c * tq, tq)
            pos = off + lax.broadcasted_iota(jnp.int32, (tq, LANES), 0)
            cols = jnp.where(lane == 0, pos >> CHUNK_SHIFT,
                             jnp.where(lane == 1, pos & (CHUNK - 1), 0))
            kaug_ref[pl.ds(off, tq), :LANES] = k_ref[pl.ds(off, tq), :]
            kaug_ref[pl.ds(off, tq), LANES:] = cols.astype(F32).astype(BF16)
            vaug_ref[pl.ds(off, tq), :LANES] = v_ref[pl.ds(off, tq), :]
            vaug_ref[pl.ds(off, tq), LANES:] = jnp.ones((tq, LANES), BF16)
            return carry

        lax.fori_loop(0, k_ref.shape[0] // tq, fill, 0)

    pos_cols = jnp.where(lane == 0, slope * CHUNK, jnp.where(lane == 1, slope, 0.0)).astype(BF16)
    for half in range(2):
        q = q_ref[half * tq:(half + 1) * tq, :].astype(F32) * (B_DH ** -0.5)
        r0 = 2 * half * tq
        qs_ref[r0:r0 + tq, :LANES] = jnp.where(lane < B_DH, q, 0.0).astype(BF16)
        qs_ref[r0 + tq:r0 + 2 * tq, :LANES] = jnp.where(lane >= B_DH, q, 0.0).astype(BF16)
        qs_ref[r0:r0 + tq, LANES:] = pos_cols
        qs_ref[r0 + tq:r0 + 2 * tq, LANES:] = pos_cols

    qk, softmax_pv, start_and_full_pairs = _flash_ops(
        qs_ref, kaug_ref, vaug_ref, sa_ref, sb_ref, m_ref, acc_ref, jnp.exp)
    start_and_full_pairs(i)
    bottom = slice(2 * tq, 4 * tq)
    diag_bias = jnp.tile(dbias_ref[...], (2, 1))
    qk(2 * i + 1, sb_ref, rows=bottom)
    softmax_pv(2 * i, sa_ref, diag_bias)
    softmax_pv(2 * i + 1, sb_ref, diag_bias, rows=bottom)

    lam = (jnp.exp(jnp.sum(lq1_ref[...] * lk1_ref[...], axis=-1, keepdims=True))
           - jnp.exp(jnp.sum(lq2_ref[...] * lk2_ref[...], axis=-1, keepdims=True))
           + lam_init)
    for half in range(2):
        r0 = 2 * half * tq
        o1 = acc_ref[r0:r0 + tq, :LANES] / acc_ref[r0:r0 + tq, LANES:]
        o2 = acc_ref[r0 + tq:r0 + 2 * tq, :LANES] / acc_ref[r0 + tq:r0 + 2 * tq, LANES:]
        o = o1 - lam * o2
        o_ref[half * tq:(half + 1) * tq, :] = (
            _rms(o, g_ref[...]) * (1.0 - lam_init)).astype(BF16)


def _diff_attn(proj, lq1, lk1, lq2, lk2, subln, *, batch, seq, lam_init):
    tq = 2 * FLASH_TK
    nq = seq // tq
    vec = pl.BlockSpec((1, B_DH), lambda b, h, i: (0, 0))
    return pl.pallas_call(
        functools.partial(_diff_attn_kernel, lam_init=lam_init),
        grid=(batch, HB, nq),
        in_specs=[
            pl.BlockSpec((tq, LANES), lambda b, h, i: (b * nq + i, h)),
            pl.BlockSpec((seq, LANES), lambda b, h, i: (b, HB + h)),
            pl.BlockSpec((seq, LANES), lambda b, h, i: (b, 2 * HB + h)),
            vec, vec, vec, vec,
            pl.BlockSpec((1, 2 * B_DH), lambda b, h, i: (0, 0)),
        ],
        out_specs=pl.BlockSpec((tq, LANES), lambda b, h, i: (b * nq + i, h)),
        out_shape=jax.ShapeDtypeStruct((batch * seq, HB * 2 * B_DH), BF16),
        scratch_shapes=[pltpu.VMEM((seq, 2 * LANES), BF16), pltpu.VMEM((seq, 2 * LANES), BF16),
                        pltpu.VMEM((FLASH_TK, FLASH_TK), F32),
                        pltpu.VMEM((2 * tq, 2 * LANES), BF16)] + _flash_scratch(2 * tq),
        compiler_params=_params(3),
        name="diff_attn",
    )(proj, proj, proj, lq1, lk1, lq2, lk2, subln)


def _band_attn_kernel(q_ref, k_ref, v_ref, base_ref, o_ref, bias_ref, s_ref, *, gq, sub):
    b = pl.program_id(1)
    g2 = pl.program_id(2)
    win = 2 * gq

    @pl.when(jnp.logical_and(b == 0, g2 == 0))
    def _():
        x = jnp.broadcast_to(base_ref[0], (gq, win))
        x = pltpu.roll(x, 0, 1, stride=1, stride_axis=0)
        d = _chunk_ids((gq, win), 1) - _chunk_ids((gq, win), 0)
        bias_ref[...] = jnp.where(jnp.logical_and(d >= 0, d <= LEFT_CHUNKS), x * LOG2E, NEG)

    def attend(half, keys, bias_cols):
        width = bias_cols.stop - bias_cols.start
        q = q_ref[half * gq:(half + 1) * gq, :].astype(F32) * (C_DH ** -0.5 * LOG2E)
        s_ref[half, :, :width] = lax.dot_general(q.astype(BF16), k_ref[keys, :], NT_DIMS,
                                                 preferred_element_type=F32)
        v = v_ref[keys, :]
        for r0 in range(0, gq, sub):
            s = s_ref[half, r0:r0 + sub, :width] + bias_ref[r0:r0 + sub, bias_cols]
            p = jnp.exp2(s - jnp.max(s, axis=-1, keepdims=True))
            l = jnp.sum(p, axis=-1, keepdims=True)
            o = jnp.dot(p.astype(BF16), v, preferred_element_type=F32)
            o_ref[half * gq + r0:half * gq + r0 + sub, :] = (o / l).astype(BF16)

    @pl.when(g2 == 0)
    def _():
        attend(0, slice(0, gq), slice(gq, win))
        attend(1, slice(0, win), slice(0, win))

    @pl.when(g2 > 0)
    def _():
        off0 = pl.multiple_of((2 * g2 - 1) * gq, gq)
        off1 = pl.multiple_of(2 * g2 * gq, gq)
        attend(0, pl.ds(off0, win), slice(0, win))
        attend(1, pl.ds(off1, win), slice(0, win))


def _band_attn(proj, base, *, batch, seq):
    gq = LEFT_CHUNKS * CHUNK
    ng = seq // (2 * gq)
    qc, kc, vc = 3 * HB, 3 * HB + HC, 3 * HB + 2 * HC
    return pl.pallas_call(
        functools.partial(_band_attn_kernel, gq=gq, sub=256),
        grid=(HC, batch, ng),
        in_specs=[
            pl.BlockSpec((2 * gq, C_DH), lambda h, b, g: (b * ng + g, qc + h)),
            pl.BlockSpec((seq, C_DH), lambda h, b, g: (b, kc + h)),
            pl.BlockSpec((seq, C_DH), lambda h, b, g: (b, vc + h)),
            pl.BlockSpec((1, 1, 2 * gq), lambda h, b, g: (h, 0, 0)),
        ],
        out_specs=pl.BlockSpec((2 * gq, C_DH), lambda h, b, g: (b * ng + g, h)),
        out_shape=jax.ShapeDtypeStruct((batch * seq, HC * C_DH), BF16),
        scratch_shapes=[pltpu.VMEM((gq, 2 * gq), F32), pltpu.VMEM((2, gq, 2 * gq), F32)],
        compiler_params=_params(3),
        name="band_attn",
    )(proj, proj, proj, base)


def _oproj_kernel(a_ref, b_ref, c_ref, x_ref, w_ref, o_ref):
    na, nb = a_ref.shape[1], a_ref.shape[1] + b_ref.shape[1]
    acc = jnp.dot(a_ref[...], w_ref[:na, :], preferred_element_type=F32)
    acc += jnp.dot(b_ref[...], w_ref[na:nb, :], preferred_element_type=F32)
    acc += jnp.dot(c_ref[...], w_ref[nb:, :], preferred_element_type=F32)
    o_ref[...] = x_ref[...] + acc


def _oproj(a, b, c, x, w, layer, *, tm=512):
    t, d = x.shape
    row = lambda i: (i, 0)
    return pl.pallas_call(
        _oproj_kernel,
        grid=(t // tm,),
        in_specs=[
            pl.BlockSpec((tm, a.shape[1]), row),
            pl.BlockSpec((tm, b.shape[1]), row),
            pl.BlockSpec((tm, c.shape[1]), row),
            pl.BlockSpec((tm, d), row),
            pl.BlockSpec((None,) + w.shape[1:], lambda i: (layer, 0, 0)),
        ],
        out_specs=pl.BlockSpec((tm, d), row),
        out_shape=jax.ShapeDtypeStruct((t, d), F32),
        compiler_params=_params(1),
        name="oproj",
    )(a, b, c, x, w)


def _mlp_kernel(x_ref, g_ref, wu_ref, wd_ref, gf_ref, o_ref, xn_ref, *, rows, final_norm):
    f = pl.program_id(1)

    def mlp_tile(xn):
        hidden = jnp.dot(xn, wu_ref[...], preferred_element_type=F32)
        hidden = jnp.square(jnp.maximum(hidden, 0.0)).astype(BF16)
        return jnp.dot(hidden, wd_ref[...], preferred_element_type=F32)

    @pl.when(f == 0)
    def _():
        half = x_ref.shape[0] // 2
        for h0 in range(0, x_ref.shape[0], half):
            for r in range(h0, h0 + half, rows):
                xn_ref[r:r + rows, :] = _rms(x_ref[r:r + rows, :], g_ref[...]).astype(BF16)
            o_ref[h0:h0 + half, :] = x_ref[h0:h0 + half, :] + mlp_tile(xn_ref[h0:h0 + half, :])

    @pl.when(f > 0)
    def _():
        o_ref[...] += mlp_tile(xn_ref[...])

    if final_norm:
        @pl.when(f == pl.num_programs(1) - 1)
        def _():
            for r in range(0, x_ref.shape[0], rows):
                o_ref[r:r + rows, :] = _rms(o_ref[r:r + rows, :], gf_ref[...])


def _mlp(x, g, wu, wd, gf, layer, *, final_norm, tm=1024, tf=MLP_TF):
    t, d = x.shape
    return pl.pallas_call(
        functools.partial(_mlp_kernel, rows=256, final_norm=final_norm),
        grid=(t // tm, wu.shape[2] // tf),
        in_specs=[
            pl.BlockSpec((tm, d), lambda i, f: (i, 0)),
            pl.BlockSpec((1, d), lambda i, f: (0, 0)),
            pl.BlockSpec((None, d, tf), lambda i, f: (layer, 0, f)),
            pl.BlockSpec((None, tf, d), lambda i, f: (layer, f, 0)),
            pl.BlockSpec((1, d), lambda i, f: (0, 0)),
        ],
        out_specs=pl.BlockSpec((tm, d), lambda i, f: (i, 0)),
        out_shape=jax.ShapeDtypeStruct((t, d), F32),
        scratch_shapes=[pltpu.VMEM((tm, d), BF16)],
        compiler_params=_params(2),
        name="mlp",
    )(x, g, wu, wd, gf)


def _rope_tables(seq):
    half = A_ROPE // 2
    lane = jnp.arange(LANES)
    inv_freq = ROPE_THETA ** (-(lane % half).astype(F32) / half)
    ang = jnp.arange(seq, dtype=F32)[:, None] * inv_freq[None, :]
    cos, sin = jnp.cos(ang), jnp.sin(ang)
    used = (lane % (LANES // 2)) < half
    return jnp.concatenate([
        jnp.where(used, cos, 0.0),
        jnp.where(used, jnp.where(lane < LANES // 2, -sin, sin), 0.0)], axis=1)


def _spread_rope(w):
    half = A_ROPE // 2
    z = jnp.zeros(w.shape[:-1] + (LANES // 2 - half,), w.dtype)
    return jnp.concatenate([w[..., :half], z, w[..., half:], z], axis=-1)


def _split_in_weight(w_in):
    lat = Q_RANK + KV_RANK
    w_lat = jnp.concatenate([w_in[:, :lat], _spread_rope(w_in[:, lat:lat + A_ROPE])], axis=1)
    w_lat = jnp.pad(w_lat, ((0, 0), (0, LAT_W - lat - LANES))).astype(BF16)
    return w_lat, w_in[:, lat + A_ROPE:].astype(BF16)


def _band_base(rel_bias):
    gq = LEFT_CHUNKS * CHUNK
    edge = rel_bias[:, REL_SIZE - 1:]
    head = jnp.broadcast_to(edge, (HC, gq - REL_CLIP))
    tail = jnp.broadcast_to(edge, (HC, 2 * gq - (gq - REL_CLIP) - REL_SIZE))
    return jnp.concatenate([head, rel_bias[:, ::-1], tail], axis=1).reshape(HC, 1, 2 * gq)


def kernel(x, attn_norm, w_in, q_a_norm, kv_a_norm, w_uq, w_ukv, lambda_q1, lambda_k1,
           lambda_q2, lambda_k2, diff_subln, rel_bias, w_o, mlp_norm, w_up, w_down, final_norm):
    batch, seq, d = x.shape
    xf = x.reshape(batch * seq, d)
    rope = _rope_tables(seq)
    row = lambda v: v.reshape(1, -1)
    wo_all, wu_all, wd_all = w_o.astype(BF16), w_up.astype(BF16), w_down.astype(BF16)
    for l in range(DEPTH):
        lat, proj = _inproj(xf, row(attn_norm[l]), *_split_in_weight(w_in[l]))

        wuq = w_uq[l].reshape(Q_RANK, HA, A_NOPE + A_ROPE)
        wuq = jnp.concatenate([wuq[..., :A_NOPE], _spread_rope(wuq[..., A_NOPE:])], axis=-1)
        wuq = wuq.reshape(Q_RANK, HA * A_QK).astype(BF16)
        wukv = w_ukv[l].reshape(KV_RANK, HA, A_NOPE + A_V)
        wukt = wukv[:, :, :A_NOPE].transpose(1, 2, 0).astype(BF16)
        wuv = wukv[:, :, A_NOPE:].transpose(1, 0, 2).astype(BF16)
        q, k, v = _mla_prep(lat, row(q_a_norm[l]), row(kv_a_norm[l]), wuq, wukt, rope, seq=seq)
        out_a = _mla_attn(q, k, v, wuv, batch=batch, seq=seq)

        lam_init = 0.8 - 0.6 * math.exp(-0.3 * l)
        out_b = _diff_attn(proj, row(lambda_q1[l]), row(lambda_k1[l]), row(lambda_q2[l]),
                           row(lambda_k2[l]), row(diff_subln[l]),
                           batch=batch, seq=seq, lam_init=lam_init)
        out_c = _band_attn(proj, _band_base(rel_bias[l]), batch=batch, seq=seq)

        xf = _oproj(out_a, out_b, out_c, xf, wo_all, l)
        xf = _mlp(xf, row(mlp_norm[l]), wu_all, wd_all, row(final_norm), l,
                  final_norm=(l == DEPTH - 1))
    return xf.reshape(batch, seq, d)
```

```python
import functools
import math

import jax
import jax.numpy as jnp
from jax import lax
from jax.experimental import pallas as pl
from jax.experimental.pallas import tpu as pltpu

F32 = jnp.float32
BF16 = jnp.bfloat16

D_MODEL = 2048
DEPTH = 2
CHUNK = 64
CHUNK_SHIFT = CHUNK.bit_length() - 1
HA, A_NOPE, A_ROPE, A_V = 8, 128, 64, 128
Q_RANK, KV_RANK = 384, 128
ROPE_THETA = 10000.0
HB, B_DH = 4, 64
HC, C_DH = 4, 128
LEFT_CHUNKS = 8
REL_CLIP = 256
REL_SIZE = REL_CLIP + CHUNK
D_FF = 4 * D_MODEL
EPS = 1e-6
NEG = -1e30

LANES = 128
LAT_W = 768
PROJ_W = 6 * 512
A_QK = 256
MLA_TQ = 256
FLASH_TK = 512
MLP_TF = 512
LOG2E = math.log2(math.e)
VMEM_LIMIT = 56 * 1024 * 1024

NT_DIMS = (((1,), (1,)), ((), ()))


def _params(n_axes, vmem=VMEM_LIMIT):
    return pltpu.CompilerParams(
        dimension_semantics=("arbitrary",) * n_axes, vmem_limit_bytes=vmem)


def _rms(x, g):
    return x * lax.rsqrt(jnp.mean(x * x, axis=-1, keepdims=True) + EPS) * g


def _chunk_ids(shape, dim):
    return lax.broadcasted_iota(jnp.int32, shape, dim) >> CHUNK_SHIFT


def _inproj_kernel(x_ref, g_ref, wl_ref, wp_ref, lat_ref, proj_ref, xn_ref, *, rows):
    for r in range(0, x_ref.shape[0], rows):
        xn_ref[r:r + rows, :] = _rms(x_ref[r:r + rows, :], g_ref[...]).astype(BF16)
        lat_ref[r:r + rows, :] = jnp.dot(xn_ref[r:r + rows, :], wl_ref[...],
                                         preferred_element_type=F32)
    for c in range(0, proj_ref.shape[1], LAT_W):
        proj_ref[:, c:c + LAT_W] = jnp.dot(xn_ref[...], wp_ref[:, c:c + LAT_W],
                                          preferred_element_type=F32).astype(BF16)


def _inproj(x, g, w_lat, w_proj, *, tm=512):
    t, d = x.shape
    const = lambda i: (0, 0)
    return pl.pallas_call(
        functools.partial(_inproj_kernel, rows=256),
        grid=(t // tm,),
        in_specs=[
            pl.BlockSpec((tm, d), lambda i: (i, 0)),
            pl.BlockSpec((1, d), const),
            pl.BlockSpec(w_lat.shape, const),
            pl.BlockSpec(w_proj.shape, const),
        ],
        out_specs=[
            pl.BlockSpec((tm, LAT_W), lambda i: (i, 0)),
            pl.BlockSpec((tm, PROJ_W), lambda i: (i, 0)),
        ],
        out_shape=[
            jax.ShapeDtypeStruct((t, LAT_W), F32),
            jax.ShapeDtypeStruct((t, PROJ_W), BF16),
        ],
        scratch_shapes=[pltpu.VMEM((tm, d), BF16)],
        compiler_params=_params(1),
        name="inproj",
    )(x, g, w_lat, w_proj)


def _mla_prep_kernel(lat_ref, qn_ref, kvn_ref, wuq_ref, wukt_ref, rope_ref,
                     q_ref, k_ref, v_ref):
    tm = lat_ref.shape[0]
    cqn = _rms(lat_ref[:, :Q_RANK], qn_ref[...]).astype(BF16)
    ckvn = _rms(lat_ref[:, Q_RANK:Q_RANK + KV_RANK], kvn_ref[...]).astype(BF16)
    kr = lat_ref[:, Q_RANK + KV_RANK:Q_RANK + KV_RANK + LANES]
    cos = rope_ref[:, :LANES]
    sin = rope_ref[:, LANES:]

    def rot(r):
        return r * cos + pltpu.roll(r, LANES // 2, 1) * sin

    scale = (A_NOPE + A_ROPE) ** -0.5 * LOG2E
    q = jnp.dot(cqn, wuq_ref[...], preferred_element_type=F32)
    for h in range(HA):
        c0 = A_QK * h
        qa = jnp.dot(q[:, c0:c0 + LANES].astype(BF16), wukt_ref[h],
                     preferred_element_type=F32)
        qa = (qa * scale).astype(BF16)
        qr = (rot(q[:, c0 + LANES:c0 + A_QK]) * scale).astype(BF16)
        for blk in range(tm // MLA_TQ):
            r0 = (blk * HA + h) * MLA_TQ
            q_ref[r0:r0 + MLA_TQ, :LANES] = qa[blk * MLA_TQ:(blk + 1) * MLA_TQ]
            q_ref[r0:r0 + MLA_TQ, LANES:] = qr[blk * MLA_TQ:(blk + 1) * MLA_TQ]
    k_ref[:, :LANES] = ckvn
    k_ref[:, LANES:] = rot(kr).astype(BF16)
    v_ref[:, :LANES] = ckvn
    v_ref[:, LANES:] = jnp.ones((tm, LANES), BF16)


def _mla_prep(lat, qn, kvn, wuq, wukt, rope, *, seq, tm=1024):
    t = lat.shape[0]
    const = lambda i: (0, 0)
    return pl.pallas_call(
        _mla_prep_kernel,
        grid=(t // tm,),
        in_specs=[
            pl.BlockSpec((tm, LAT_W), lambda i: (i, 0)),
            pl.BlockSpec((1, Q_RANK), const),
            pl.BlockSpec((1, KV_RANK), const),
            pl.BlockSpec(wuq.shape, const),
            pl.BlockSpec(wukt.shape, lambda i: (0, 0, 0)),
            pl.BlockSpec((tm, 2 * LANES), lambda i: (i % (seq // tm), 0)),
        ],
        out_specs=[
            pl.BlockSpec((tm * HA, A_QK), lambda i: (i, 0)),
            pl.BlockSpec((tm, A_QK), lambda i: (i, 0)),
            pl.BlockSpec((tm, A_QK), lambda i: (i, 0)),
        ],
        out_shape=[
            jax.ShapeDtypeStruct((t * HA, A_QK), BF16),
            jax.ShapeDtypeStruct((t, A_QK), BF16),
            jax.ShapeDtypeStruct((t, A_QK), BF16),
        ],
        compiler_params=_params(1),
        name="mla_prep",
    )(lat, qn, kvn, wuq, wukt, rope)


def _flash_ops(q_ref, k_ref, v_ref, sa_ref, sb_ref, m_ref, acc_ref, exp_fn):
    tk = sa_ref.shape[1]
    everything = slice(None)

    def qk(t, dst_ref, rows=everything):
        off = pl.multiple_of(t * tk, tk)
        dst_ref[rows, :] = lax.dot_general(q_ref[rows, :], k_ref[pl.ds(off, tk), :], NT_DIMS,
                                           preferred_element_type=F32)

    def softmax_pv(t, src_ref, bias=None, rows=everything):
        s = src_ref[rows, :]
        if bias is not None:
            nb = bias.shape[0]
            s = jnp.concatenate([s[:nb] + bias, s[nb:]], axis=0) if nb < s.shape[0] else s + bias
        m_prev = m_ref[rows, :]
        m_next = jnp.maximum(m_prev, jnp.max(s, axis=1)[:, None])
        p = exp_fn(s - jnp.tile(m_next, (1, tk // LANES)))
        alpha = exp_fn(m_prev - m_next)
        off = pl.multiple_of(t * tk, tk)
        pv = jnp.dot(p.astype(BF16), v_ref[pl.ds(off, tk), :], preferred_element_type=F32)
        acc_ref[rows, :] = jnp.tile(alpha, (1, 2)) * acc_ref[rows, :] + pv
        m_ref[rows, :] = m_next

    def pair(t):
        qk(t + 1, sb_ref)
        softmax_pv(t, sa_ref)
        qk(t + 2, sa_ref)
        softmax_pv(t + 1, sb_ref)

    def start_and_full_pairs(npairs):
        m_ref[...] = jnp.full(m_ref.shape, NEG, F32)
        acc_ref[...] = jnp.zeros(acc_ref.shape, F32)
        qk(0, sa_ref)

        def quad(j, carry):
            pair(4 * j)
            pair(4 * j + 2)
            return carry

        lax.fori_loop(0, npairs >> 1, quad, 0)

        @pl.when((npairs & 1) != 0)
        def _():
            pair(2 * (npairs - 1))

    return qk, softmax_pv, start_and_full_pairs


def _flash_scratch(rows):
    return [pltpu.VMEM((rows, FLASH_TK), F32), pltpu.VMEM((rows, FLASH_TK), F32),
            pltpu.VMEM((rows, LANES), F32), pltpu.VMEM((rows, 2 * LANES), F32)]


def _mla_attn_kernel(q_ref, k_ref, v_ref, wuv_ref, o_ref, sa_ref, sb_ref, m_ref, acc_ref):
    i = pl.program_id(1)
    n = (i * MLA_TQ) // FLASH_TK

    def last_bias():
        shape = (MLA_TQ, FLASH_TK)
        rows = i * MLA_TQ + lax.broadcasted_iota(jnp.int32, shape, 0)
        cols = n * FLASH_TK + lax.broadcasted_iota(jnp.int32, shape, 1)
        bias = jnp.where((cols >> CHUNK_SHIFT) <= (rows >> CHUNK_SHIFT), 0.0, NEG)
        return jnp.tile(bias, (HA, 1))

    qk, softmax_pv, start_and_full_pairs = _flash_ops(
        q_ref, k_ref, v_ref, sa_ref, sb_ref, m_ref, acc_ref, jnp.exp2)
    start_and_full_pairs(n >> 1)

    @pl.when((n & 1) == 0)
    def _():
        softmax_pv(n, sa_ref, last_bias())

    @pl.when((n & 1) == 1)
    def _():
        qk(n, sb_ref)
        softmax_pv(n - 1, sa_ref)
        softmax_pv(n, sb_ref, last_bias())

    for h in range(HA):
        a = acc_ref[h * MLA_TQ:(h + 1) * MLA_TQ, :]
        o = (a[:, :LANES] / a[:, LANES:]).astype(BF16)
        o_ref[:, h * A_V:(h + 1) * A_V] = jnp.dot(
            o, wuv_ref[h], preferred_element_type=F32).astype(BF16)


def _mla_attn(q, k, v, wuv, *, batch, seq):
    nq = seq // MLA_TQ
    rows = HA * MLA_TQ
    return pl.pallas_call(
        _mla_attn_kernel,
        grid=(batch, nq),
        in_specs=[
            pl.BlockSpec((rows, A_QK), lambda b, i: (b * nq + i, 0)),
            pl.BlockSpec((seq, A_QK), lambda b, i: (b, 0)),
            pl.BlockSpec((seq, A_QK), lambda b, i: (b, 0)),
            pl.BlockSpec(wuv.shape, lambda b, i: (0, 0, 0)),
        ],
        out_specs=pl.BlockSpec((MLA_TQ, HA * A_V), lambda b, i: (b * nq + i, 0)),
        out_shape=jax.ShapeDtypeStruct((batch * seq, HA * A_V), BF16),
        scratch_shapes=_flash_scratch(rows),
        compiler_params=_params(2),
        name="mla_attn",
    )(q, k, v, wuv)


def _diff_attn_kernel(q_ref, k_ref, v_ref, lq1_ref, lk1_ref, lq2_ref, lk2_ref, g_ref, o_ref,
                      kaug_ref, vaug_ref, dbias_ref, qs_ref, sa_ref, sb_ref, m_ref, acc_ref,
                      *, lam_init):
    h = pl.program_id(1)
    i = pl.program_id(2)
    tq = FLASH_TK
    lane = lax.broadcasted_iota(jnp.int32, (tq, LANES), 1)
    slope_bits = (127 - (8 // HB) * (h + 1)) << 23
    slope = lax.bitcast_convert_type(jnp.full((1, LANES), slope_bits, jnp.int32), F32)

    @pl.when(i == 0)
    def _():
        r = lax.broadcasted_iota(jnp.int32, (tq, tq), 0)
        c = lax.broadcasted_iota(jnp.int32, (tq, tq), 1)
        ahead = jnp.maximum(c - r, 0).astype(F32)
        bias = jnp.tile(slope, (1, tq // LANES)) * (-2.0 * ahead)
        dbias_ref[...] = jnp.where((c >> CHUNK_SHIFT) <= (r >> CHUNK_SHIFT), bias, NEG)

        def fill(c, carry):
            off = pl.multiple_of(c * tq, tq)
            pos = off + lax.broadcasted_iota(jnp.int32, (tq, LANES), 0)
            cols = jnp.where(lane == 0, pos >> CHUNK_SHIFT,
                             jnp.where(lane == 1, pos & (CHUNK - 1), 0))
            kaug_ref[pl.ds(off, tq), :LANES] = k_ref[pl.ds(off, tq), :]
            kaug_ref[pl.ds(off, tq), LANES:] = cols.astype(F32).astype(BF16)
            vaug_ref[pl.ds(off, tq), :LANES] = v_ref[pl.ds(off, tq), :]
            vaug_ref[pl.ds(off, tq), LANES:] = jnp.ones((tq, LANES), BF16)
            return carry

        lax.fori_loop(0, k_ref.shape[0] // tq, fill, 0)

    pos_cols = jnp.where(lane == 0, slope * CHUNK, jnp.where(lane == 1, slope, 0.0)).astype(BF16)
    for half in range(2):
        q = q_ref[half * tq:(half + 1) * tq, :].astype(F32) * (B_DH ** -0.5)
        r0 = 2 * half * tq
        qs_ref[r0:r0 + tq, :LANES] = jnp.where(lane < B_DH, q, 0.0).astype(BF16)
        qs_ref[r0 + tq:r0 + 2 * tq, :LANES] = jnp.where(lane >= B_DH, q, 0.0).astype(BF16)
        qs_ref[r0:r0 + tq, LANES:] = pos_cols
        qs_ref[r0 + tq:r0 + 2 * tq, LANES:] = pos_cols

    qk, softmax_pv, start_and_full_pairs = _flash_ops(
        qs_ref, kaug_ref, vaug_ref, sa_ref, sb_ref, m_ref, acc_ref, jnp.exp)
    start_and_full_pairs(i)
    bottom = slice(2 * tq, 4 * tq)
    diag_bias = jnp.tile(dbias_ref[...], (2, 1))
    qk(2 * i + 1, sb_ref, rows=bottom)
    softmax_pv(2 * i, sa_ref, diag_bias)
    softmax_pv(2 * i + 1, sb_ref, diag_bias, rows=bottom)

    lam = (jnp.exp(jnp.sum(lq1_ref[...] * lk1_ref[...], axis=-1, keepdims=True))
           - jnp.exp(jnp.sum(lq2_ref[...] * lk2_ref[...], axis=-1, keepdims=True))
           + lam_init)
    for half in range(2):
        r0 = 2 * half * tq
        o1 = acc_ref[r0:r0 + tq, :LANES] / acc_ref[r0:r0 + tq, LANES:]
        o2 = acc_ref[r0 + tq:r0 + 2 * tq, :LANES] / acc_ref[r0 + tq:r0 + 2 * tq, LANES:]
        o = o1 - lam * o2
        o_ref[half * tq:(half + 1) * tq, :] = (
            _rms(o, g_ref[...]) * (1.0 - lam_init)).astype(BF16)


def _diff_attn(proj, lq1, lk1, lq2, lk2, subln, *, batch, seq, lam_init):
    tq = 2 * FLASH_TK
    nq = seq // tq
    vec = pl.BlockSpec((1, B_DH), lambda b, h, i: (0, 0))
    return pl.pallas_call(
        functools.partial(_diff_attn_kernel, lam_init=lam_init),
        grid=(batch, HB, nq),
        in_specs=[
            pl.BlockSpec((tq, LANES), lambda b, h, i: (b * nq + i, h)),
            pl.BlockSpec((seq, LANES), lambda b, h, i: (b, HB + h)),
            pl.BlockSpec((seq, LANES), lambda b, h, i: (b, 2 * HB + h)),
            vec, vec, vec, vec,
            pl.BlockSpec((1, 2 * B_DH), lambda b, h, i: (0, 0)),
        ],
        out_specs=pl.BlockSpec((tq, LANES), lambda b, h, i: (b * nq + i, h)),
        out_shape=jax.ShapeDtypeStruct((batch * seq, HB * 2 * B_DH), BF16),
        scratch_shapes=[pltpu.VMEM((seq, 2 * LANES), BF16), pltpu.VMEM((seq, 2 * LANES), BF16),
                        pltpu.VMEM((FLASH_TK, FLASH_TK), F32),
                        pltpu.VMEM((2 * tq, 2 * LANES), BF16)] + _flash_scratch(2 * tq),
        compiler_params=_params(3),
        name="diff_attn",
    )(proj, proj, proj, lq1, lk1, lq2, lk2, subln)


def _band_attn_kernel(q_ref, k_ref, v_ref, base_ref, o_ref, bias_ref, s_ref, *, gq, sub):
    b = pl.program_id(1)
    g2 = pl.program_id(2)
    win = 2 * gq

    @pl.when(jnp.logical_and(b == 0, g2 == 0))
    def _():
        x = jnp.broadcast_to(base_ref[0], (gq, win))
        x = pltpu.roll(x, 0, 1, stride=1, stride_axis=0)
        d = _chunk_ids((gq, win), 1) - _chunk_ids((gq, win), 0)
        bias_ref[...] = jnp.where(jnp.logical_and(d >= 0, d <= LEFT_CHUNKS), x * LOG2E, NEG)

    def attend(half, keys, bias_cols):
        width = bias_cols.stop - bias_cols.start
        q = q_ref[half * gq:(half + 1) * gq, :].astype(F32) * (C_DH ** -0.5 * LOG2E)
        s_ref[half, :, :width] = lax.dot_general(q.astype(BF16), k_ref[keys, :], NT_DIMS,
                                                 preferred_element_type=F32)
        v = v_ref[keys, :]
        for r0 in range(0, gq, sub):
            s = s_ref[half, r0:r0 + sub, :width] + bias_ref[r0:r0 + sub, bias_cols]
            p = jnp.exp2(s - jnp.max(s, axis=-1, keepdims=True))
            l = jnp.sum(p, axis=-1, keepdims=True)
            o = jnp.dot(p.astype(BF16), v, preferred_element_type=F32)
            o_ref[half * gq + r0:half * gq + r0 + sub, :] = (o / l).astype(BF16)

    @pl.when(g2 == 0)
    def _():
        attend(0, slice(0, gq), slice(gq, win))
        attend(1, slice(0, win), slice(0, win))

    @pl.when(g2 > 0)
    def _():
        off0 = pl.multiple_of((2 * g2 - 1) * gq, gq)
        off1 = pl.multiple_of(2 * g2 * gq, gq)
        attend(0, pl.ds(off0, win), slice(0, win))
        attend(1, pl.ds(off1, win), slice(0, win))


def _band_attn(proj, base, *, batch, seq):
    gq = LEFT_CHUNKS * CHUNK
    ng = seq // (2 * gq)
    qc, kc, vc = 3 * HB, 3 * HB + HC, 3 * HB + 2 * HC
    return pl.pallas_call(
        functools.partial(_band_attn_kernel, gq=gq, sub=256),
        grid=(HC, batch, ng),
        in_specs=[
            pl.BlockSpec((2 * gq, C_DH), lambda h, b, g: (b * ng + g, qc + h)),
            pl.BlockSpec((seq, C_DH), lambda h, b, g: (b, kc + h)),
            pl.BlockSpec((seq, C_DH), lambda h, b, g: (b, vc + h)),
            pl.BlockSpec((1, 1, 2 * gq), lambda h, b, g: (h, 0, 0)),
        ],
        out_specs=pl.BlockSpec((2 * gq, C_DH), lambda h, b, g: (b * ng + g, h)),
        out_shape=jax.ShapeDtypeStruct((batch * seq, HC * C_DH), BF16),
        scratch_shapes=[pltpu.VMEM((gq, 2 * gq), F32), pltpu.VMEM((2, gq, 2 * gq), F32)],
        compiler_params=_params(3),
        name="band_attn",
    )(proj, proj, proj, base)


def _oproj_kernel(a_ref, b_ref, c_ref, x_ref, w_ref, o_ref):
    na, nb = a_ref.shape[1], a_ref.shape[1] + b_ref.shape[1]
    acc = jnp.dot(a_ref[...], w_ref[:na, :], preferred_element_type=F32)
    acc += jnp.dot(b_ref[...], w_ref[na:nb, :], preferred_element_type=F32)
    acc += jnp.dot(c_ref[...], w_ref[nb:, :], preferred_element_type=F32)
    o_ref[...] = x_ref[...] + acc


def _oproj(a, b, c, x, w, layer, *, tm=512):
    t, d = x.shape
    row = lambda i: (i, 0)
    return pl.pallas_call(
        _oproj_kernel,
        grid=(t // tm,),
        in_specs=[
            pl.BlockSpec((tm, a.shape[1]), row),
            pl.BlockSpec((tm, b.shape[1]), row),
            pl.BlockSpec((tm, c.shape[1]), row),
            pl.BlockSpec((tm, d), row),
            pl.BlockSpec((None,) + w.shape[1:], lambda i: (layer, 0, 0)),
        ],
        out_specs=pl.BlockSpec((tm, d), row),
        out_shape=jax.ShapeDtypeStruct((t, d), F32),
        compiler_params=_params(1),
        name="oproj",
    )(a, b, c, x, w)


def _mlp_kernel(x_ref, g_ref, wu_ref, wd_ref, gf_ref, o_ref, xn_ref, *, rows, final_norm):
    f = pl.program_id(1)

    def mlp_tile(xn):
        hidden = jnp.dot(xn, wu_ref[...], preferred_element_type=F32)
        hidden = jnp.square(jnp.maximum(hidden, 0.0)).astype(BF16)
        return jnp.dot(hidden, wd_ref[...], preferred_element_type=F32)

    @pl.when(f == 0)
    def _():
        half = x_ref.shape[0] // 2
        for h0 in range(0, x_ref.shape[0], half):
            for r in range(h0, h0 + half, rows):
                xn_ref[r:r + rows, :] = _rms(x_ref[r:r + rows, :], g_ref[...]).astype(BF16)
            o_ref[h0:h0 + half, :] = x_ref[h0:h0 + half, :] + mlp_tile(xn_ref[h0:h0 + half, :])

    last = pl.num_programs(1) - 1

    @pl.when(jnp.logical_and(f > 0, f < last) if final_norm else f > 0)
    def _():
        o_ref[...] += mlp_tile(xn_ref[...])

    if final_norm:
        @pl.when(f == last)
        def _():
            half = x_ref.shape[0] // 2
            for h0 in range(0, x_ref.shape[0], half):
                o_ref[h0:h0 + half, :] += mlp_tile(xn_ref[h0:h0 + half, :])
                for r in range(h0, h0 + half, rows):
                    o_ref[r:r + rows, :] = _rms(o_ref[r:r + rows, :], gf_ref[...])


def _mlp(x, g, wu, wd, gf, layer, *, final_norm, tm=1024, tf=MLP_TF):
    t, d = x.shape
    return pl.pallas_call(
        functools.partial(_mlp_kernel, rows=256, final_norm=final_norm),
        grid=(t // tm, wu.shape[2] // tf),
        in_specs=[
            pl.BlockSpec((tm, d), lambda i, f: (i, 0)),
            pl.BlockSpec((1, d), lambda i, f: (0, 0)),
            pl.BlockSpec((None, d, tf), lambda i, f: (layer, 0, f)),
            pl.BlockSpec((None, tf, d), lambda i, f: (layer, f, 0)),
            pl.BlockSpec((1, d), lambda i, f: (0, 0)),
        ],
        out_specs=pl.BlockSpec((tm, d), lambda i, f: (i, 0)),
        out_shape=jax.ShapeDtypeStruct((t, d), F32),
        scratch_shapes=[pltpu.VMEM((tm, d), BF16)],
        compiler_params=_params(2),
        name="mlp",
    )(x, g, wu, wd, gf)


def _rope_tables(seq):
    half = A_ROPE // 2
    lane = jnp.arange(LANES)
    inv_freq = ROPE_THETA ** (-(lane % half).astype(F32) / half)
    ang = jnp.arange(seq, dtype=F32)[:, None] * inv_freq[None, :]
    cos, sin = jnp.cos(ang), jnp.sin(ang)
    used = (lane % (LANES // 2)) < half
    return jnp.concatenate([
        jnp.where(used, cos, 0.0),
        jnp.where(used, jnp.where(lane < LANES // 2, -sin, sin), 0.0)], axis=1)


def _spread_rope(w):
    half = A_ROPE // 2
    z = jnp.zeros(w.shape[:-1] + (LANES // 2 - half,), w.dtype)
    return jnp.concatenate([w[..., :half], z, w[..., half:], z], axis=-1)


def _split_in_weight(w_in):
    lat = Q_RANK + KV_RANK
    w_lat = jnp.concatenate([w_in[:, :lat], _spread_rope(w_in[:, lat:lat + A_ROPE])], axis=1)
    w_lat = jnp.pad(w_lat, ((0, 0), (0, LAT_W - lat - LANES))).astype(BF16)
    return w_lat, w_in[:, lat + A_ROPE:].astype(BF16)


def _band_base(rel_bias):
    gq = LEFT_CHUNKS * CHUNK
    edge = rel_bias[:, REL_SIZE - 1:]
    head = jnp.broadcast_to(edge, (HC, gq - REL_CLIP))
    tail = jnp.broadcast_to(edge, (HC, 2 * gq - (gq - REL_CLIP) - REL_SIZE))
    return jnp.concatenate([head, rel_bias[:, ::-1], tail], axis=1).reshape(HC, 1, 2 * gq)


def kernel(x, attn_norm, w_in, q_a_norm, kv_a_norm, w_uq, w_ukv, lambda_q1, lambda_k1,
           lambda_q2, lambda_k2, diff_subln, rel_bias, w_o, mlp_norm, w_up, w_down, final_norm):
    batch, seq, d = x.shape
    xf = x.reshape(batch * seq, d)
    rope = _rope_tables(seq)
    row = lambda v: v.reshape(1, -1)
    wo_all, wu_all, wd_all = w_o.astype(BF16), w_up.astype(BF16), w_down.astype(BF16)
    for l in range(DEPTH):
        lat, proj = _inproj(xf, row(attn_norm[l]), *_split_in_weight(w_in[l]))

        wuq = w_uq[l].reshape(Q_RANK, HA, A_NOPE + A_ROPE)
        wuq = jnp.concatenate([wuq[..., :A_NOPE], _spread_rope(wuq[..., A_NOPE:])], axis=-1)
        wuq = wuq.reshape(Q_RANK, HA * A_QK).astype(BF16)
        wukv = w_ukv[l].reshape(KV_RANK, HA, A_NOPE + A_V)
        wukt = wukv[:, :, :A_NOPE].transpose(1, 2, 0).astype(BF16)
        wuv = wukv[:, :, A_NOPE:].transpose(1, 0, 2).astype(BF16)
        q, k, v = _mla_prep(lat, row(q_a_norm[l]), row(kv_a_norm[l]), wuq, wukt, rope, seq=seq)
        out_a = _mla_attn(q, k, v, wuv, batch=batch, seq=seq)

        lam_init = 0.8 - 0.6 * math.exp(-0.3 * l)
        out_b = _diff_attn(proj, row(lambda_q1[l]), row(lambda_k1[l]), row(lambda_q2[l]),
                           row(lambda_k2[l]), row(diff_subln[l]),
                           batch=batch, seq=seq, lam_init=lam_init)
        out_c = _band_attn(proj, _band_base(rel_bias[l]), batch=batch, seq=seq)

        xf = _oproj(out_a, out_b, out_c, xf, wo_all, l)
        xf = _mlp(xf, row(mlp_norm[l]), wu_all, wd_all, row(final_norm), l,
                  final_norm=(l == DEPTH - 1))
    return xf.reshape(batch, seq, d)
```

```python
import functools
import math

import jax
import jax.numpy as jnp
from jax import lax
from jax.experimental import pallas as pl
from jax.experimental.pallas import tpu as pltpu

F32 = jnp.float32
BF16 = jnp.bfloat16

D_MODEL = 2048
DEPTH = 2
CHUNK = 64
CHUNK_SHIFT = CHUNK.bit_length() - 1
HA, A_NOPE, A_ROPE, A_V = 8, 128, 64, 128
Q_RANK, KV_RANK = 384, 128
ROPE_THETA = 10000.0
HB, B_DH = 4, 64
HC, C_DH = 4, 128
LEFT_CHUNKS = 8
REL_CLIP = 256
REL_SIZE = REL_CLIP + CHUNK
D_FF = 4 * D_MODEL
EPS = 1e-6
NEG = -1e30

LANES = 128
LAT_W = 768
PROJ_W = 6 * 512
A_QK = 256
MLA_TQ = 256
FLASH_TK = 512
MLP_TF = 1024
LOG2E = math.log2(math.e)
VMEM_LIMIT = 56 * 1024 * 1024

NT_DIMS = (((1,), (1,)), ((), ()))


def _params(n_axes, vmem=VMEM_LIMIT):
    return pltpu.CompilerParams(
        dimension_semantics=("arbitrary",) * n_axes, vmem_limit_bytes=vmem)


def _rms(x, g):
    return x * lax.rsqrt(jnp.mean(x * x, axis=-1, keepdims=True) + EPS) * g


def _chunk_ids(shape, dim):
    return lax.broadcasted_iota(jnp.int32, shape, dim) >> CHUNK_SHIFT


def _inproj_kernel(x_ref, g_ref, wl_ref, wp_ref, lat_ref, proj_ref, xn_ref, *, rows):
    for r in range(0, x_ref.shape[0], rows):
        xn_ref[r:r + rows, :] = _rms(x_ref[r:r + rows, :], g_ref[...]).astype(BF16)
        lat_ref[r:r + rows, :] = jnp.dot(xn_ref[r:r + rows, :], wl_ref[...],
                                         preferred_element_type=F32)
    for c in range(0, proj_ref.shape[1], LAT_W):
        proj_ref[:, c:c + LAT_W] = jnp.dot(xn_ref[...], wp_ref[:, c:c + LAT_W],
                                          preferred_element_type=F32).astype(BF16)


def _inproj(x, g, w_lat, w_proj, *, tm=512):
    t, d = x.shape
    const = lambda i: (0, 0)
    return pl.pallas_call(
        functools.partial(_inproj_kernel, rows=256),
        grid=(t // tm,),
        in_specs=[
            pl.BlockSpec((tm, d), lambda i: (i, 0)),
            pl.BlockSpec((1, d), const),
            pl.BlockSpec(w_lat.shape, const),
            pl.BlockSpec(w_proj.shape, const),
        ],
        out_specs=[
            pl.BlockSpec((tm, LAT_W), lambda i: (i, 0)),
            pl.BlockSpec((tm, PROJ_W), lambda i: (i, 0)),
        ],
        out_shape=[
            jax.ShapeDtypeStruct((t, LAT_W), F32),
            jax.ShapeDtypeStruct((t, PROJ_W), BF16),
        ],
        scratch_shapes=[pltpu.VMEM((tm, d), BF16)],
        compiler_params=_params(1),
        name="inproj",
    )(x, g, w_lat, w_proj)


def _mla_prep_kernel(lat_ref, qn_ref, kvn_ref, wuq_ref, wukt_ref, rope_ref,
                     q_ref, k_ref, v_ref):
    tm = lat_ref.shape[0]
    cqn = _rms(lat_ref[:, :Q_RANK], qn_ref[...]).astype(BF16)
    ckvn = _rms(lat_ref[:, Q_RANK:Q_RANK + KV_RANK], kvn_ref[...]).astype(BF16)
    kr = lat_ref[:, Q_RANK + KV_RANK:Q_RANK + KV_RANK + LANES]
    cos = rope_ref[:, :LANES]
    sin = rope_ref[:, LANES:]

    def rot(r):
        return r * cos + pltpu.roll(r, LANES // 2, 1) * sin

    scale = (A_NOPE + A_ROPE) ** -0.5 * LOG2E
    q = jnp.dot(cqn, wuq_ref[...], preferred_element_type=F32)
    for h in range(HA):
        c0 = A_QK * h
        qa = jnp.dot(q[:, c0:c0 + LANES].astype(BF16), wukt_ref[h],
                     preferred_element_type=F32)
        qa = (qa * scale).astype(BF16)
        qr = (rot(q[:, c0 + LANES:c0 + A_QK]) * scale).astype(BF16)
        for blk in range(tm // MLA_TQ):
            r0 = (blk * HA + h) * MLA_TQ
            q_ref[r0:r0 + MLA_TQ, :LANES] = qa[blk * MLA_TQ:(blk + 1) * MLA_TQ]
            q_ref[r0:r0 + MLA_TQ, LANES:] = qr[blk * MLA_TQ:(blk + 1) * MLA_TQ]
    k_ref[:, :LANES] = ckvn
    k_ref[:, LANES:] = rot(kr).astype(BF16)
    v_ref[:, :LANES] = ckvn
    v_ref[:, LANES:] = jnp.ones((tm, LANES), BF16)


def _mla_prep(lat, qn, kvn, wuq, wukt, rope, *, seq, tm=1024):
    t = lat.shape[0]
    const = lambda i: (0, 0)
    return pl.pallas_call(
        _mla_prep_kernel,
        grid=(t // tm,),
        in_specs=[
            pl.BlockSpec((tm, LAT_W), lambda i: (i, 0)),
            pl.BlockSpec((1, Q_RANK), const),
            pl.BlockSpec((1, KV_RANK), const),
            pl.BlockSpec(wuq.shape, const),
            pl.BlockSpec(wukt.shape, lambda i: (0, 0, 0)),
            pl.BlockSpec((tm, 2 * LANES), lambda i: (i % (seq // tm), 0)),
        ],
        out_specs=[
            pl.BlockSpec((tm * HA, A_QK), lambda i: (i, 0)),
            pl.BlockSpec((tm, A_QK), lambda i: (i, 0)),
            pl.BlockSpec((tm, A_QK), lambda i: (i, 0)),
        ],
        out_shape=[
            jax.ShapeDtypeStruct((t * HA, A_QK), BF16),
            jax.ShapeDtypeStruct((t, A_QK), BF16),
            jax.ShapeDtypeStruct((t, A_QK), BF16),
        ],
        compiler_params=_params(1),
        name="mla_prep",
    )(lat, qn, kvn, wuq, wukt, rope)


def _flash_ops(q_ref, k_ref, v_ref, sa_ref, sb_ref, m_ref, acc_ref, exp_fn):
    tk = sa_ref.shape[1]
    everything = slice(None)

    def qk(t, dst_ref, rows=everything):
        off = pl.multiple_of(t * tk, tk)
        dst_ref[rows, :] = lax.dot_general(q_ref[rows, :], k_ref[pl.ds(off, tk), :], NT_DIMS,
                                           preferred_element_type=F32)

    def softmax_pv(t, src_ref, bias=None, rows=everything):
        s = src_ref[rows, :]
        if bias is not None:
            nb = bias.shape[0]
            s = jnp.concatenate([s[:nb] + bias, s[nb:]], axis=0) if nb < s.shape[0] else s + bias
        m_prev = m_ref[rows, :]
        m_next = jnp.maximum(m_prev, jnp.max(s, axis=1)[:, None])
        p = exp_fn(s - jnp.tile(m_next, (1, tk // LANES)))
        alpha = exp_fn(m_prev - m_next)
        off = pl.multiple_of(t * tk, tk)
        pv = jnp.dot(p.astype(BF16), v_ref[pl.ds(off, tk), :], preferred_element_type=F32)
        acc_ref[rows, :] = jnp.tile(alpha, (1, 2)) * acc_ref[rows, :] + pv
        m_ref[rows, :] = m_next

    def pair(t):
        qk(t + 1, sb_ref)
        softmax_pv(t, sa_ref)
        qk(t + 2, sa_ref)
        softmax_pv(t + 1, sb_ref)

    def start_and_full_pairs(npairs):
        m_ref[...] = jnp.full(m_ref.shape, NEG, F32)
        acc_ref[...] = jnp.zeros(acc_ref.shape, F32)
        qk(0, sa_ref)

        def quad(j, carry):
            pair(4 * j)
            pair(4 * j + 2)
            return carry

        lax.fori_loop(0, npairs >> 1, quad, 0)

        @pl.when((npairs & 1) != 0)
        def _():
            pair(2 * (npairs - 1))

    return qk, softmax_pv, start_and_full_pairs


def _flash_scratch(rows):
    return [pltpu.VMEM((rows, FLASH_TK), F32), pltpu.VMEM((rows, FLASH_TK), F32),
            pltpu.VMEM((rows, LANES), F32), pltpu.VMEM((rows, 2 * LANES), F32)]


def _mla_attn_kernel(q_ref, k_ref, v_ref, wuv_ref, o_ref, sa_ref, sb_ref, m_ref, acc_ref):
    i = pl.program_id(1)
    n = (i * MLA_TQ) // FLASH_TK

    def last_bias():
        shape = (MLA_TQ, FLASH_TK)
        rows = i * MLA_TQ + lax.broadcasted_iota(jnp.int32, shape, 0)
        cols = n * FLASH_TK + lax.broadcasted_iota(jnp.int32, shape, 1)
        bias = jnp.where((cols >> CHUNK_SHIFT) <= (rows >> CHUNK_SHIFT), 0.0, NEG)
        return jnp.tile(bias, (HA, 1))

    qk, softmax_pv, start_and_full_pairs = _flash_ops(
        q_ref, k_ref, v_ref, sa_ref, sb_ref, m_ref, acc_ref, jnp.exp2)
    start_and_full_pairs(n >> 1)

    @pl.when((n & 1) == 0)
    def _():
        softmax_pv(n, sa_ref, last_bias())

    @pl.when((n & 1) == 1)
    def _():
        qk(n, sb_ref)
        softmax_pv(n - 1, sa_ref)
        softmax_pv(n, sb_ref, last_bias())

    for h in range(HA):
        a = acc_ref[h * MLA_TQ:(h + 1) * MLA_TQ, :]
        o = (a[:, :LANES] / a[:, LANES:]).astype(BF16)
        o_ref[:, h * A_V:(h + 1) * A_V] = jnp.dot(
            o, wuv_ref[h], preferred_element_type=F32).astype(BF16)


def _mla_attn(q, k, v, wuv, *, batch, seq):
    nq = seq // MLA_TQ
    rows = HA * MLA_TQ
    return pl.pallas_call(
        _mla_attn_kernel,
        grid=(batch, nq),
        in_specs=[
            pl.BlockSpec((rows, A_QK), lambda b, i: (b * nq + i, 0)),
            pl.BlockSpec((seq, A_QK), lambda b, i: (b, 0)),
            pl.BlockSpec((seq, A_QK), lambda b, i: (b, 0)),
            pl.BlockSpec(wuv.shape, lambda b, i: (0, 0, 0)),
        ],
        out_specs=pl.BlockSpec((MLA_TQ, HA * A_V), lambda b, i: (b * nq + i, 0)),
        out_shape=jax.ShapeDtypeStruct((batch * seq, HA * A_V), BF16),
        scratch_shapes=_flash_scratch(rows),
        compiler_params=_params(2),
        name="mla_attn",
    )(q, k, v, wuv)


def _diff_attn_kernel(q_ref, k_ref, v_ref, lq1_ref, lk1_ref, lq2_ref, lk2_ref, g_ref, o_ref,
                      kaug_ref, vaug_ref, dbias_ref, qs_ref, sa_ref, sb_ref, m_ref, acc_ref,
                      *, lam_init):
    h = pl.program_id(1)
    i = pl.program_id(2)
    tq = FLASH_TK
    lane = lax.broadcasted_iota(jnp.int32, (tq, LANES), 1)
    slope_bits = (127 - (8 // HB) * (h + 1)) << 23
    slope = lax.bitcast_convert_type(jnp.full((1, LANES), slope_bits, jnp.int32), F32)

    @pl.when(i == 0)
    def _():
        r = lax.broadcasted_iota(jnp.int32, (tq, tq), 0)
        c = lax.broadcasted_iota(jnp.int32, (tq, tq), 1)
        ahead = jnp.maximum(c - r, 0).astype(F32)
        bias = jnp.tile(slope, (1, tq // LANES)) * (-2.0 * ahead)
        dbias_ref[...] = jnp.where((c >> CHUNK_SHIFT) <= (r >> CHUNK_SHIFT), bias, NEG)

        def fill(c, carry):
            off = pl.multiple_of(c * tq, tq)
            pos = off + lax.broadcasted_iota(jnp.int32, (tq, LANES), 0)
            cols = jnp.where(lane == 0, pos >> CHUNK_SHIFT,
                             jnp.where(lane == 1, pos & (CHUNK - 1), 0))
            kaug_ref[pl.ds(off, tq), :LANES] = k_ref[pl.ds(off, tq), :]
            kaug_ref[pl.ds(off, tq), LANES:] = cols.astype(F32).astype(BF16)
            vaug_ref[pl.ds(off, tq), :LANES] = v_ref[pl.ds(off, tq), :]
            vaug_ref[pl.ds(off, tq), LANES:] = jnp.ones((tq, LANES), BF16)
            return carry

        lax.fori_loop(0, k_ref.shape[0] // tq, fill, 0)

    pos_cols = jnp.where(lane == 0, slope * CHUNK, jnp.where(lane == 1, slope, 0.0)).astype(BF16)
    for half in range(2):
        q = q_ref[half * tq:(half + 1) * tq, :].astype(F32) * (B_DH ** -0.5)
        r0 = 2 * half * tq
        qs_ref[r0:r0 + tq, :LANES] = jnp.where(lane < B_DH, q, 0.0).astype(BF16)
        qs_ref[r0 + tq:r0 + 2 * tq, :LANES] = jnp.where(lane >= B_DH, q, 0.0).astype(BF16)
        qs_ref[r0:r0 + tq, LANES:] = pos_cols
        qs_ref[r0 + tq:r0 + 2 * tq, LANES:] = pos_cols

    qk, softmax_pv, start_and_full_pairs = _flash_ops(
        qs_ref, kaug_ref, vaug_ref, sa_ref, sb_ref, m_ref, acc_ref, jnp.exp)
    start_and_full_pairs(i)
    bottom = slice(2 * tq, 4 * tq)
    diag_bias = jnp.tile(dbias_ref[...], (2, 1))
    qk(2 * i + 1, sb_ref, rows=bottom)
    softmax_pv(2 * i, sa_ref, diag_bias)
    softmax_pv(2 * i + 1, sb_ref, diag_bias, rows=bottom)

    lam = (jnp.exp(jnp.sum(lq1_ref[...] * lk1_ref[...], axis=-1, keepdims=True))
           - jnp.exp(jnp.sum(lq2_ref[...] * lk2_ref[...], axis=-1, keepdims=True))
           + lam_init)
    for half in range(2):
        r0 = 2 * half * tq
        o1 = acc_ref[r0:r0 + tq, :LANES] / acc_ref[r0:r0 + tq, LANES:]
        o2 = acc_ref[r0 + tq:r0 + 2 * tq, :LANES] / acc_ref[r0 + tq:r0 + 2 * tq, LANES:]
        o = o1 - lam * o2
        o_ref[half * tq:(half + 1) * tq, :] = (
            _rms(o, g_ref[...]) * (1.0 - lam_init)).astype(BF16)


def _diff_attn(proj, lq1, lk1, lq2, lk2, subln, *, batch, seq, lam_init):
    tq = 2 * FLASH_TK
    nq = seq // tq
    vec = pl.BlockSpec((1, B_DH), lambda b, h, i: (0, 0))
    return pl.pallas_call(
        functools.partial(_diff_attn_kernel, lam_init=lam_init),
        grid=(batch, HB, nq),
        in_specs=[
            pl.BlockSpec((tq, LANES), lambda b, h, i: (b * nq + i, h)),
            pl.BlockSpec((seq, LANES), lambda b, h, i: (b, HB + h)),
            pl.BlockSpec((seq, LANES), lambda b, h, i: (b, 2 * HB + h)),
            vec, vec, vec, vec,
            pl.BlockSpec((1, 2 * B_DH), lambda b, h, i: (0, 0)),
        ],
        out_specs=pl.BlockSpec((tq, LANES), lambda b, h, i: (b * nq + i, h)),
        out_shape=jax.ShapeDtypeStruct((batch * seq, HB * 2 * B_DH), BF16),
        scratch_shapes=[pltpu.VMEM((seq, 2 * LANES), BF16), pltpu.VMEM((seq, 2 * LANES), BF16),
                        pltpu.VMEM((FLASH_TK, FLASH_TK), F32),
                        pltpu.VMEM((2 * tq, 2 * LANES), BF16)] + _flash_scratch(2 * tq),
        compiler_params=_params(3),
        name="diff_attn",
    )(proj, proj, proj, lq1, lk1, lq2, lk2, subln)


def _band_attn_kernel(q_ref, k_ref, v_ref, base_ref, o_ref, bias_ref, s_ref, *, gq, sub):
    b = pl.program_id(1)
    g2 = pl.program_id(2)
    win = 2 * gq

    @pl.when(jnp.logical_and(b == 0, g2 == 0))
    def _():
        x = jnp.broadcast_to(base_ref[0], (gq, win))
        x = pltpu.roll(x, 0, 1, stride=1, stride_axis=0)
        d = _chunk_ids((gq, win), 1) - _chunk_ids((gq, win), 0)
        bias_ref[...] = jnp.where(jnp.logical_and(d >= 0, d <= LEFT_CHUNKS), x * LOG2E, NEG)

    def attend(half, keys, bias_cols):
        width = bias_cols.stop - bias_cols.start
        q = q_ref[half * gq:(half + 1) * gq, :].astype(F32) * (C_DH ** -0.5 * LOG2E)
        s_ref[half, :, :width] = lax.dot_general(q.astype(BF16), k_ref[keys, :], NT_DIMS,
                                                 preferred_element_type=F32)
        v = v_ref[keys, :]
        for r0 in range(0, gq, sub):
            s = s_ref[half, r0:r0 + sub, :width] + bias_ref[r0:r0 + sub, bias_cols]
            p = jnp.exp2(s - jnp.max(s, axis=-1, keepdims=True))
            l = jnp.sum(p, axis=-1, keepdims=True)
            o = jnp.dot(p.astype(BF16), v, preferred_element_type=F32)
            o_ref[half * gq + r0:half * gq + r0 + sub, :] = (o / l).astype(BF16)

    @pl.when(g2 == 0)
    def _():
        attend(0, slice(0, gq), slice(gq, win))
        attend(1, slice(0, win), slice(0, win))

    @pl.when(g2 > 0)
    def _():
        off0 = pl.multiple_of((2 * g2 - 1) * gq, gq)
        off1 = pl.multiple_of(2 * g2 * gq, gq)
        attend(0, pl.ds(off0, win), slice(0, win))
        attend(1, pl.ds(off1, win), slice(0, win))


def _band_attn(proj, base, *, batch, seq):
    gq = LEFT_CHUNKS * CHUNK
    ng = seq // (2 * gq)
    qc, kc, vc = 3 * HB, 3 * HB + HC, 3 * HB + 2 * HC
    return pl.pallas_call(
        functools.partial(_band_attn_kernel, gq=gq, sub=256),
        grid=(HC, batch, ng),
        in_specs=[
            pl.BlockSpec((2 * gq, C_DH), lambda h, b, g: (b * ng + g, qc + h)),
            pl.BlockSpec((seq, C_DH), lambda h, b, g: (b, kc + h)),
            pl.BlockSpec((seq, C_DH), lambda h, b, g: (b, vc + h)),
            pl.BlockSpec((1, 1, 2 * gq), lambda h, b, g: (h, 0, 0)),
        ],
        out_specs=pl.BlockSpec((2 * gq, C_DH), lambda h, b, g: (b * ng + g, h)),
        out_shape=jax.ShapeDtypeStruct((batch * seq, HC * C_DH), BF16),
        scratch_shapes=[pltpu.VMEM((gq, 2 * gq), F32), pltpu.VMEM((2, gq, 2 * gq), F32)],
        compiler_params=_params(3),
        name="band_attn",
    )(proj, proj, proj, base)


def _oproj_kernel(a_ref, b_ref, c_ref, x_ref, w_ref, o_ref):
    na, nb = a_ref.shape[1], a_ref.shape[1] + b_ref.shape[1]
    acc = jnp.dot(a_ref[...], w_ref[:na, :], preferred_element_type=F32)
    acc += jnp.dot(b_ref[...], w_ref[na:nb, :], preferred_element_type=F32)
    acc += jnp.dot(c_ref[...], w_ref[nb:, :], preferred_element_type=F32)
    o_ref[...] = x_ref[...] + acc


def _oproj(a, b, c, x, w, layer, *, tm=512):
    t, d = x.shape
    row = lambda i: (i, 0)
    return pl.pallas_call(
        _oproj_kernel,
        grid=(t // tm,),
        in_specs=[
            pl.BlockSpec((tm, a.shape[1]), row),
            pl.BlockSpec((tm, b.shape[1]), row),
            pl.BlockSpec((tm, c.shape[1]), row),
            pl.BlockSpec((tm, d), row),
            pl.BlockSpec((None,) + w.shape[1:], lambda i: (layer, 0, 0)),
        ],
        out_specs=pl.BlockSpec((tm, d), row),
        out_shape=jax.ShapeDtypeStruct((t, d), F32),
        compiler_params=_params(1),
        name="oproj",
    )(a, b, c, x, w)


def _mlp_kernel(x_ref, g_ref, wu_ref, wd_ref, gf_ref, o_ref, xn_ref, *, rows, final_norm):
    f = pl.program_id(1)

    def mlp_tile(xn):
        hidden = jnp.dot(xn, wu_ref[...], preferred_element_type=F32)
        hidden = jnp.square(jnp.maximum(hidden, 0.0)).astype(BF16)
        return jnp.dot(hidden, wd_ref[...], preferred_element_type=F32)

    @pl.when(f == 0)
    def _():
        half = x_ref.shape[0] // 2
        for h0 in range(0, x_ref.shape[0], half):
            for r in range(h0, h0 + half, rows):
                xn_ref[r:r + rows, :] = _rms(x_ref[r:r + rows, :], g_ref[...]).astype(BF16)
            o_ref[h0:h0 + half, :] = x_ref[h0:h0 + half, :] + mlp_tile(xn_ref[h0:h0 + half, :])

    @pl.when(f > 0)
    def _():
        o_ref[...] += mlp_tile(xn_ref[...])

    if final_norm:
        @pl.when(f == pl.num_programs(1) - 1)
        def _():
            for r in range(0, x_ref.shape[0], rows):
                o_ref[r:r + rows, :] = _rms(o_ref[r:r + rows, :], gf_ref[...])


def _mlp(x, g, wu, wd, gf, layer, *, final_norm, tm=1024, tf=MLP_TF):
    t, d = x.shape
    return pl.pallas_call(
        functools.partial(_mlp_kernel, rows=256, final_norm=final_norm),
        grid=(t // tm, wu.shape[2] // tf),
        in_specs=[
            pl.BlockSpec((tm, d), lambda i, f: (i, 0), pipeline_mode=pl.Buffered(1)),
            pl.BlockSpec((1, d), lambda i, f: (0, 0)),
            pl.BlockSpec((None, d, tf), lambda i, f: (layer, 0, f)),
            pl.BlockSpec((None, tf, d), lambda i, f: (layer, f, 0)),
            pl.BlockSpec((1, d), lambda i, f: (0, 0)),
        ],
        out_specs=pl.BlockSpec((tm, d), lambda i, f: (i, 0)),
        out_shape=jax.ShapeDtypeStruct((t, d), F32),
        scratch_shapes=[pltpu.VMEM((tm, d), BF16)],
        compiler_params=_params(2),
        name="mlp",
    )(x, g, wu, wd, gf)


def _rope_tables(seq):
    half = A_ROPE // 2
    lane = jnp.arange(LANES)
    inv_freq = ROPE_THETA ** (-(lane % half).astype(F32) / half)
    ang = jnp.arange(seq, dtype=F32)[:, None] * inv_freq[None, :]
    cos, sin = jnp.cos(ang), jnp.sin(ang)
    used = (lane % (LANES // 2)) < half
    return jnp.concatenate([
        jnp.where(used, cos, 0.0),
        jnp.where(used, jnp.where(lane < LANES // 2, -sin, sin), 0.0)], axis=1)


def _spread_rope(w):
    half = A_ROPE // 2
    z = jnp.zeros(w.shape[:-1] + (LANES // 2 - half,), w.dtype)
    return jnp.concatenate([w[..., :half], z, w[..., half:], z], axis=-1)


def _split_in_weight(w_in):
    lat = Q_RANK + KV_RANK
    w_lat = jnp.concatenate([w_in[:, :lat], _spread_rope(w_in[:, lat:lat + A_ROPE])], axis=1)
    w_lat = jnp.pad(w_lat, ((0, 0), (0, LAT_W - lat - LANES))).astype(BF16)
    return w_lat, w_in[:, lat + A_ROPE:].astype(BF16)


def _band_base(rel_bias):
    gq = LEFT_CHUNKS * CHUNK
    edge = rel_bias[:, REL_SIZE - 1:]
    head = jnp.broadcast_to(edge, (HC, gq - REL_CLIP))
    tail = jnp.broadcast_to(edge, (HC, 2 * gq - (gq - REL_CLIP) - REL_SIZE))
    return jnp.concatenate([head, rel_bias[:, ::-1], tail], axis=1).reshape(HC, 1, 2 * gq)


def kernel(x, attn_norm, w_in, q_a_norm, kv_a_norm, w_uq, w_ukv, lambda_q1, lambda_k1,
           lambda_q2, lambda_k2, diff_subln, rel_bias, w_o, mlp_norm, w_up, w_down, final_norm):
    batch, seq, d = x.shape
    xf = x.reshape(batch * seq, d)
    rope = _rope_tables(seq)
    row = lambda v: v.reshape(1, -1)
    wo_all, wu_all, wd_all = w_o.astype(BF16), w_up.astype(BF16), w_down.astype(BF16)
    for l in range(DEPTH):
        lat, proj = _inproj(xf, row(attn_norm[l]), *_split_in_weight(w_in[l]))

        wuq = w_uq[l].reshape(Q_RANK, HA, A_NOPE + A_ROPE)
        wuq = jnp.concatenate([wuq[..., :A_NOPE], _spread_rope(wuq[..., A_NOPE:])], axis=-1)
        wuq = wuq.reshape(Q_RANK, HA * A_QK).astype(BF16)
        wukv = w_ukv[l].reshape(KV_RANK, HA, A_NOPE + A_V)
        wukt = wukv[:, :, :A_NOPE].transpose(1, 2, 0).astype(BF16)
        wuv = wukv[:, :, A_NOPE:].transpose(1, 0, 2).astype(BF16)
        q, k, v = _mla_prep(lat, row(q_a_norm[l]), row(kv_a_norm[l]), wuq, wukt, rope, seq=seq)
        out_a = _mla_attn(q, k, v, wuv, batch=batch, seq=seq)

        lam_init = 0.8 - 0.6 * math.exp(-0.3 * l)
        out_b = _diff_attn(proj, row(lambda_q1[l]), row(lambda_k1[l]), row(lambda_q2[l]),
                           row(lambda_k2[l]), row(diff_subln[l]),
                           batch=batch, seq=seq, lam_init=lam_init)
        out_c = _band_attn(proj, _band_base(rel_bias[l]), batch=batch, seq=seq)

        xf = _oproj(out_a, out_b, out_c, xf, wo_all, l)
        xf = _mlp(xf, row(mlp_norm[l]), wu_all, wd_all, row(final_norm), l,
                  final_norm=(l == DEPTH - 1))
    return xf.reshape(batch, seq, d)
```
